```python
import jax
import jax.numpy as jnp
from jax import lax
import numpy as np

D_MODEL = 2048
BATCH = 4
SEQ = 8192
DEPTH = 1

CHUNK = 64
N_HEADS_M = 4
HEAD_DIM_M = 256
D_M = N_HEADS_M * HEAD_DIM_M
N_HEADS_SB = 8
HEAD_DIM_SB = 128
D_SB = N_HEADS_SB * HEAD_DIM_SB
CONV_W = 4
D_FF = 5632
D_PLE = 256
Q_BLOCK = 128
ALPHA = (2.0 * DEPTH) ** 0.25
BETA = (8.0 * DEPTH) ** -0.25
LN_EPS = 1e-5
NEG_BIG = -1e30
F_BIAS_LO = 3.0
F_BIAS_HI = 6.0
IN_COLS = 4 * D_M + 2 * N_HEADS_M + 3 * D_SB + 2 * D_MODEL

kernel_name = "hybrid_mlstm_stickbreaking_macaron_deepnorm"


def _layer_norm(x, g, b):
    xf = x.astype(jnp.float32)
    mu = jnp.mean(xf, axis=-1, keepdims=True)
    var = jnp.mean(jnp.square(xf - mu), axis=-1, keepdims=True)
    return ((xf - mu) * lax.rsqrt(var + LN_EPS) * g + b).astype(x.dtype)


def _head_norm(h, g):
    mu = jnp.mean(h, axis=-1, keepdims=True)
    var = jnp.mean(jnp.square(h - mu), axis=-1, keepdims=True)
    return (h - mu) * lax.rsqrt(var + LN_EPS) * g


def _swiglu(x, w1, w3, w2):
    return (jax.nn.silu(x @ w1) * (x @ w3)) @ w2


def _causal_conv(x, w):
    s = x.shape[1]
    xp = jnp.pad(x, ((0, 0), (CONV_W - 1, 0), (0, 0)))
    out = xp[:, 0:s] * w[0]
    for j in range(1, CONV_W):
        out = out + xp[:, j:j + s] * w[j]
    return out


def _split_cols(proj):
    sizes = [D_M, D_M, D_M, D_M, N_HEADS_M, N_HEADS_M, D_SB, D_SB, D_SB, D_MODEL, D_MODEL]
    offs = np.cumsum(sizes)[:-1].tolist()
    return jnp.split(proj, offs, axis=-1)


def _heads(t, nh):
    b, s, _ = t.shape
    return t.reshape(b, s, nh, -1).transpose(0, 2, 1, 3).astype(jnp.float32)


def _mlstm_chunkwise(q, k, v, log_i, log_f):
    bsz, nh, s, dh = q.shape
    nc = s // CHUNK

    def to_chunks(t):
        return jnp.moveaxis(t.reshape(bsz, nh, nc, CHUNK, *t.shape[3:]), 2, 0)

    xs = (to_chunks(q), to_chunks(k), to_chunks(v), to_chunks(log_i), to_chunks(log_f))
    causal = jnp.tril(jnp.ones((CHUNK, CHUNK), dtype=bool))

    def step(carry, inp):
        c_prev, n_prev, m_prev = carry
        qc, kc, vc, ic, fc = inp
        b = jnp.cumsum(fc, axis=-1)
        d_log = jnp.where(causal, b[..., :, None] - b[..., None, :] + ic[..., None, :], NEG_BIG)
        inter_log = b + m_prev[..., None]
        m_t = jnp.maximum(jnp.max(d_log, axis=-1), inter_log)
        scores = jnp.einsum('bhtk,bhsk->bhts', qc, kc) * jnp.exp(d_log - m_t[..., None])
        inter_scale = jnp.exp(inter_log - m_t)
        num = (jnp.einsum('bhts,bhsv->bhtv', scores, vc)
               + inter_scale[..., None] * jnp.einsum('bhtk,bhvk->bhtv', qc, c_prev))
        den = jnp.sum(scores, axis=-1) + inter_scale * jnp.einsum('bhtk,bhk->bht', qc, n_prev)
        h = num / jnp.maximum(jnp.abs(den), jnp.exp(-m_t))[..., None]
        g = b[..., -1]
        w_log = g[..., None] - b + ic
        m_new = jnp.maximum(g + m_prev, jnp.max(w_log, axis=-1))
        w = jnp.exp(w_log - m_new[..., None])
        decay = jnp.exp(g + m_prev - m_new)
        c_new = decay[..., None, None] * c_prev + jnp.einsum('bhs,bhsv,bhsk->bhvk', w, vc, kc)
        n_new = decay[..., None] * n_prev + jnp.einsum('bhs,bhsk->bhk', w, kc)
        return (c_new, n_new, m_new), h

    init = (jnp.zeros((bsz, nh, dh, dh), jnp.float32),
            jnp.zeros((bsz, nh, dh), jnp.float32),
            jnp.full((bsz, nh), NEG_BIG, jnp.float32))
    _, h = lax.scan(step, init, xs)
    return jnp.moveaxis(h, 0, 2).reshape(bsz, nh, s, dh)


def _stick_breaking(q, k, v):
    s, dh = q.shape[2], q.shape[3]
    scale = dh ** -0.5
    outs = []
    for t0 in range(0, s, Q_BLOCK):
        end = t0 + Q_BLOCK
        z = jnp.einsum('bhtd,bhsd->bhts', q[:, :, t0:end], k[:, :, :end]) * scale
        tpos = t0 + jnp.arange(Q_BLOCK)
        spos = jnp.arange(end)
        strict = spos[None, :] < tpos[:, None]
        log_one_minus = jnp.where(strict, jax.nn.log_sigmoid(-z), 0.0)
        cum = jnp.cumsum(log_one_minus, axis=-1)
        rem = cum[..., -1:] - cum
        att = jnp.where(strict, jnp.exp(jax.nn.log_sigmoid(z) + rem), 0.0)
        outs.append(jnp.einsum('bhts,bhsd->bhtd', att, v[:, :, :end]))
    return jnp.concatenate(outs, axis=2)


def _hybrid_mixer(x, w_in, b_gates, conv_w, norm_g, w_up_m, w_up_sb, w_out):
    bsz, s, _ = x.shape
    proj = x @ w_in
    mq, mk, mv, mo, mi, mf, sq, sk, sv, ga, gb = _split_cols(proj)
    qk = jax.nn.silu(_causal_conv(jnp.concatenate([mq, mk], axis=-1), conv_w))
    q_m, k_m = jnp.split(qk, 2, axis=-1)
    log_i = (mi + b_gates[:N_HEADS_M]).astype(jnp.float32).transpose(0, 2, 1)
    log_f = jax.nn.log_sigmoid((mf + b_gates[N_HEADS_M:]).astype(jnp.float32)).transpose(0, 2, 1)
    h = _mlstm_chunkwise(_heads(q_m, N_HEADS_M),
                         _heads(k_m, N_HEADS_M) * (HEAD_DIM_M ** -0.5),
                         _heads(mv, N_HEADS_M), log_i, log_f)
    h = _head_norm(h.transpose(0, 2, 1, 3), norm_g.astype(jnp.float32).reshape(N_HEADS_M, HEAD_DIM_M))
    y_m = (h.reshape(bsz, s, D_M) * jax.nn.sigmoid(mo.astype(jnp.float32))).astype(x.dtype)
    y_sb = _stick_breaking(_heads(sq, N_HEADS_SB), _heads(sk, N_HEADS_SB), _heads(sv, N_HEADS_SB))
    y_sb = y_sb.transpose(0, 2, 1, 3).reshape(bsz, s, D_SB).astype(x.dtype)
    merged = jax.nn.sigmoid(ga) * (y_m @ w_up_m) + jax.nn.sigmoid(gb) * (y_sb @ w_up_sb)
    return merged @ w_out


def setup_inputs(seed: int = 0) -> dict:
    key = jax.random.key(seed)
    ks = jax.random.split(key, 24)
    f32 = jnp.float32

    def nrm(k, shape, scale):
        return jax.random.normal(k, shape, f32) * scale

    col_scale = jnp.concatenate([
        jnp.ones((2 * D_M,), f32), jnp.full((D_M,), BETA, f32),
        jnp.ones((D_M + 2 * N_HEADS_M + 2 * D_SB,), f32), jnp.full((D_SB,), BETA, f32),
        jnp.ones((2 * D_MODEL,), f32)])
    f_bias = jnp.linspace(F_BIAS_LO, F_BIAS_HI, N_HEADS_M, dtype=f32)
    b_gates_m = jnp.concatenate([nrm(ks[5], (DEPTH, N_HEADS_M), 0.1),
                                 f_bias[None, :] + nrm(ks[6], (DEPTH, N_HEADS_M), 0.1)], axis=-1)
    return {
        "x": nrm(ks[0], (BATCH, SEQ, D_MODEL), 1.0),
        "p": nrm(ks[1], (DEPTH, BATCH, SEQ, D_PLE), 1.0),
        "ffn1_w1": nrm(ks[2], (DEPTH, D_MODEL, D_FF), D_MODEL ** -0.5),
        "ffn1_w3": nrm(ks[3], (DEPTH, D_MODEL, D_FF), D_MODEL ** -0.5),
        "ffn1_w2": nrm(ks[4], (DEPTH, D_FF, D_MODEL), BETA * D_FF ** -0.5),
        "ln1_g": 1.0 + nrm(ks[7], (DEPTH, D_MODEL), 0.02),
        "ln1_b": nrm(ks[8], (DEPTH, D_MODEL), 0.02),
        "w_in": nrm(ks[9], (DEPTH, D_MODEL, IN_COLS), D_MODEL ** -0.5) * col_scale,
        "b_gates_m": b_gates_m,
        "conv_m": nrm(ks[10], (DEPTH, CONV_W, 2 * D_M), CONV_W ** -0.5),
        "norm_m": 1.0 + nrm(ks[11], (DEPTH, D_M), 0.02),
        "w_up_m": nrm(ks[12], (DEPTH, D_M, D_MODEL), BETA * D_M ** -0.5),
        "w_up_sb": nrm(ks[13], (DEPTH, D_SB, D_MODEL), BETA * D_SB ** -0.5),
        "w_out": nrm(ks[14], (DEPTH, D_MODEL, D_MODEL), BETA * D_MODEL ** -0.5),
        "ln2_g": 1.0 + nrm(ks[15], (DEPTH, D_MODEL), 0.02),
        "ln2_b": nrm(ks[16], (DEPTH, D_MODEL), 0.02),
        "ffn2_w1": nrm(ks[17], (DEPTH, D_MODEL, D_FF), D_MODEL ** -0.5),
        "ffn2_w3": nrm(ks[18], (DEPTH, D_MODEL, D_FF), D_MODEL ** -0.5),
        "ffn2_w2": nrm(ks[19], (DEPTH, D_FF, D_MODEL), BETA * D_FF ** -0.5),
        "ln3_g": 1.0 + nrm(ks[20], (DEPTH, D_MODEL), 0.02),
        "ln3_b": nrm(ks[21], (DEPTH, D_MODEL), 0.02),
        "w_ple_gate": nrm(ks[22], (DEPTH, D_MODEL, D_MODEL), D_MODEL ** -0.5),
        "w_ple_proj": nrm(ks[23], (DEPTH, D_PLE, D_MODEL), D_PLE ** -0.5),
    }


def reference(x, p, ffn1_w1, ffn1_w3, ffn1_w2, ln1_g, ln1_b, w_in, b_gates_m, conv_m, norm_m,
              w_up_m, w_up_sb, w_out, ln2_g, ln2_b, ffn2_w1, ffn2_w3, ffn2_w2, ln3_g, ln3_b,
              w_ple_gate, w_ple_proj):
    for i in range(DEPTH):
        x = _layer_norm(ALPHA * x + 0.5 * _swiglu(x, ffn1_w1[i], ffn1_w3[i], ffn1_w2[i]), ln1_g[i], ln1_b[i])
        mix = _hybrid_mixer(x, w_in[i], b_gates_m[i], conv_m[i], norm_m[i], w_up_m[i], w_up_sb[i], w_out[i])
        x = _layer_norm(ALPHA * x + mix, ln2_g[i], ln2_b[i])
        x = _layer_norm(ALPHA * x + 0.5 * _swiglu(x, ffn2_w1[i], ffn2_w3[i], ffn2_w2[i]), ln3_g[i], ln3_b[i])
        x = x + jax.nn.sigmoid(x @ w_ple_gate[i]) * (p[i] @ w_ple_proj[i])
    return x
```

```python
import functools

import jax
import jax.numpy as jnp
from jax import lax
from jax.experimental import pallas as pl
from jax.experimental.pallas import tpu as pltpu

D_MODEL = 2048
DEPTH = 1
N_HEADS_M = 4
HEAD_DIM_M = 256
D_M = N_HEADS_M * HEAD_DIM_M
N_HEADS_SB = 8
HEAD_DIM_SB = 128
D_SB = N_HEADS_SB * HEAD_DIM_SB
CONV_W = 4
D_FF = 5632
D_PLE = 256
ALPHA = (2.0 * DEPTH) ** 0.25
LN_EPS = 1e-5
NEG_BIG = -1e30

LANES = 128
SUBLANES = 8
VMEM_LIMIT_BYTES = 56 * 1024 * 1024

PROJ_COLS = 2 * D_MODEL + 4 * D_M + 3 * D_SB
GATE_COLS = LANES

FFN_TM = 512
FFN_TF = 512
PROJ_TM = 1024
PROJ_TN = 1024
MLSTM_CHUNK = 256
SB_TQ = 256
MERGE_TM = 256
PLE_TM = 512

F32 = jnp.float32
BF16 = jnp.bfloat16


def _sigmoid(x):
    return 1.0 / (1.0 + jnp.exp(-x))


def _log_sigmoid(x):
    return jnp.minimum(x, 0.0) - jnp.log(1.0 + jnp.exp(-jnp.abs(x)))


def _layer_norm(y, g, b):
    mu = jnp.mean(y, axis=-1, keepdims=True)
    yc = y - mu
    var = jnp.mean(yc * yc, axis=-1, keepdims=True)
    return yc * lax.rsqrt(var + LN_EPS) * g + b


def _split_bf16(x):
    hi = x.astype(BF16)
    lo = (x - hi.astype(F32)).astype(BF16)
    return hi, lo


def _dot(a, b):
    return jnp.dot(a, b, preferred_element_type=F32)


def _dot_nt(a, b):
    return lax.dot_general(a, b, (((1,), (1,)), ((), ())), preferred_element_type=F32)


def _dot_tn(a, b):
    return lax.dot_general(a, b, (((0,), (0,)), ((), ())), preferred_element_type=F32)


def _ffn_ln_body(x_ref, w1_ref, w3_ref, w2_ref, g_ref, b_ref, o_ref, xb_ref, acc_ref):
    k = pl.program_id(1)

    @pl.when(k == 0)
    def _():
        xb_ref[...] = x_ref[...].astype(BF16)
        acc_ref[...] = jnp.zeros_like(acc_ref)

    xb = xb_ref[...]
    h1 = _dot(xb, w1_ref[...])
    h3 = _dot(xb, w3_ref[...])
    h = (h1 * _sigmoid(h1)) * h3
    acc_ref[...] += _dot(h.astype(BF16), w2_ref[...])

    @pl.when(k == pl.num_programs(1) - 1)
    def _():
        y = ALPHA * x_ref[...] + 0.5 * acc_ref[...]
        o_ref[...] = _layer_norm(y, g_ref[...], b_ref[...])


def _ffn_ln(x, w1, w3, w2, g, b):
    t = x.shape[0]
    grid = (t // FFN_TM, D_FF // FFN_TF)
    return pl.pallas_call(
        _ffn_ln_body,
        grid=grid,
        in_specs=[
            pl.BlockSpec((FFN_TM, D_MODEL), lambda i, k: (i, 0)),
            pl.BlockSpec((D_MODEL, FFN_TF), lambda i, k: (0, k)),
            pl.BlockSpec((D_MODEL, FFN_TF), lambda i, k: (0, k)),
            pl.BlockSpec((FFN_TF, D_MODEL), lambda i, k: (k, 0)),
            pl.BlockSpec((1, D_MODEL), lambda i, k: (0, 0)),
            pl.BlockSpec((1, D_MODEL), lambda i, k: (0, 0)),
        ],
        out_specs=pl.BlockSpec((FFN_TM, D_MODEL), lambda i, k: (i, 0)),
        out_shape=jax.ShapeDtypeStruct((t, D_MODEL), F32),
        scratch_shapes=[pltpu.VMEM((FFN_TM, D_MODEL), BF16), pltpu.VMEM((FFN_TM, D_MODEL), F32)],
        compiler_params=pltpu.CompilerParams(
            dimension_semantics=("parallel", "arbitrary"), vmem_limit_bytes=VMEM_LIMIT_BYTES),
        name="ffn_ln",
    )(x, w1, w3, w2, g, b)


def _in_proj_body(x_ref, w_ref, wg_ref, o_ref, g_ref, xb_ref):
    j = pl.program_id(1)

    @pl.when(j == 0)
    def _():
        xb = x_ref[...].astype(BF16)
        xb_ref[...] = xb
        g_ref[...] = _dot(xb, wg_ref[...])

    o_ref[...] = _dot(xb_ref[...], w_ref[...]).astype(BF16)


def _in_proj(x, w_main, w_gates):
    t = x.shape[0]
    grid = (t // PROJ_TM, PROJ_COLS // PROJ_TN)
    return pl.pallas_call(
        _in_proj_body,
        grid=grid,
        in_specs=[
            pl.BlockSpec((PROJ_TM, D_MODEL), lambda i, j: (i, 0)),
            pl.BlockSpec((D_MODEL, PROJ_TN), lambda i, j: (0, j)),
            pl.BlockSpec((D_MODEL, GATE_COLS), lambda i, j: (0, 0)),
        ],
        out_specs=[
            pl.BlockSpec((PROJ_TM, PROJ_TN), lambda i, j: (i, j)),
            pl.BlockSpec((PROJ_TM, GATE_COLS), lambda i, j: (i, 0)),
        ],
        out_shape=[
            jax.ShapeDtypeStruct((t, PROJ_COLS), BF16),
            jax.ShapeDtypeStruct((t, GATE_COLS), F32),
        ],
        scratch_shapes=[pltpu.VMEM((PROJ_TM, D_MODEL), BF16)],
        compiler_params=pltpu.CompilerParams(
            dimension_semantics=("parallel", "arbitrary"), vmem_limit_bytes=VMEM_LIMIT_BYTES),
        name="in_proj",
    )(x, w_main, w_gates)


def _mlstm_body(q_ref, k_ref, v_ref, og_ref, g_ref, bg_ref, conv_ref, norm_ref, y_ref,
                cbuf_ref, c_ref, n_ref, m_ref):
    ln = MLSTM_CHUNK
    dh = HEAD_DIM_M

    @pl.when(pl.program_id(1) == 0)
    def _():
        cbuf_ref[0:SUBLANES, :] = jnp.zeros((SUBLANES, 2 * D_M), F32)
        c_ref[...] = jnp.zeros_like(c_ref)
        n_ref[...] = jnp.zeros_like(n_ref)
        m_ref[...] = jnp.full_like(m_ref, NEG_BIG)

    cbuf_ref[SUBLANES:SUBLANES + ln, 0:D_M] = q_ref[...].astype(F32)
    cbuf_ref[SUBLANES:SUBLANES + ln, D_M:2 * D_M] = k_ref[...].astype(F32)

    def conv_silu(col0):
        base = SUBLANES - (CONV_W - 1)
        acc = cbuf_ref[base:base + ln, col0:col0 + dh] * conv_ref[0:1, col0:col0 + dh]
        for j in range(1, CONV_W):
            acc = acc + cbuf_ref[base + j:base + j + ln, col0:col0 + dh] * conv_ref[j:j + 1, col0:col0 + dh]
        return acc * _sigmoid(acc)

    gall = g_ref[...] + bg_ref[...]
    lane = lax.broadcasted_iota(jnp.int32, (ln, GATE_COLS), 1)
    is_f = (lane >= N_HEADS_M) & (lane < 2 * N_HEADS_M)
    lf = jnp.where(is_f, _log_sigmoid(gall), 0.0)
    row = lax.broadcasted_iota(jnp.int32, (ln, ln), 0)
    col = lax.broadcasted_iota(jnp.int32, (ln, ln), 1)
    causal = col <= row
    tri = jnp.where(causal, 1.0, 0.0).astype(BF16)
    lf_hi, lf_lo = _split_bf16(lf)
    b_all = _dot(tri, lf_hi) + _dot(tri, lf_lo)
    b_all_t = b_all.T
    gall_t = gall.T

    for h in range(N_HEADS_M):
        hs = slice(h * dh, (h + 1) * dh)
        qh = conv_silu(h * dh)
        kh = conv_silu(D_M + h * dh) * (HEAD_DIM_M ** -0.5)
        qb = qh.astype(BF16)
        kb = kh.astype(BF16)
        vb = v_ref[:, hs]

        b_col = b_all[:, N_HEADS_M + h:N_HEADS_M + h + 1]
        li_col = gall[:, h:h + 1]
        b_row = b_all_t[N_HEADS_M + h:N_HEADS_M + h + 1, :]
        li_row = gall_t[h:h + 1, :]
        m_prev = m_ref[h:h + 1, 0:1]

        d_log = jnp.where(causal, (b_col - b_row) + li_row, NEG_BIG)
        inter_log = b_col + m_prev
        m_t = jnp.maximum(jnp.max(d_log, axis=-1, keepdims=True), inter_log)
        scores = _dot_nt(qb, kb) * jnp.exp(d_log - m_t)
        inter_scale = jnp.exp(inter_log - m_t)
        c_prev = c_ref[h]
        n_prev = n_ref[h]
        num = _dot(scores.astype(BF16), vb) + inter_scale * _dot(qb, c_prev.astype(BF16))
        den = (jnp.sum(scores, axis=-1, keepdims=True)
               + inter_scale * jnp.sum(qh * n_prev, axis=-1, keepdims=True))
        hval = num / jnp.maximum(jnp.abs(den), jnp.exp(-m_t))

        g_last = b_col[ln - 1:ln, :]
        w_log = (g_last - b_col) + li_col
        m_new = jnp.maximum(g_last + m_prev, jnp.max(w_log, axis=0, keepdims=True))
        decay = jnp.exp((g_last + m_prev) - m_new)
        kw = kh * jnp.exp(w_log - m_new)
        c_ref[h] = decay * c_prev + _dot_tn(kw.astype(BF16), vb)
        n_ref[h] = decay * n_prev + jnp.sum(kw, axis=0, keepdims=True)
        m_ref[h:h + 1, :] = jnp.broadcast_to(m_new, (1, LANES))

        mu = jnp.mean(hval, axis=-1, keepdims=True)
        hc = hval - mu
        var = jnp.mean(hc * hc, axis=-1, keepdims=True)
        hn = hc * lax.rsqrt(var + LN_EPS) * norm_ref[:, hs]
        y_ref[:, hs] = (hn * _sigmoid(og_ref[:, hs].astype(F32))).astype(BF16)

    cbuf_ref[0:SUBLANES, :] = cbuf_ref[ln:ln + SUBLANES, :]


def _mlstm(proj, gates, b_gates, conv_w, norm_g, bsz, seq):
    ln = MLSTM_CHUNK
    n_chunks = seq // ln
    col_blk = 2 * D_MODEL // D_M
    return pl.pallas_call(
        _mlstm_body,
        grid=(bsz, n_chunks),
        in_specs=[
            pl.BlockSpec((None, ln, D_M), lambda b, c: (b, c, col_blk)),
            pl.BlockSpec((None, ln, D_M), lambda b, c: (b, c, col_blk + 1)),
            pl.BlockSpec((None, ln, D_M), lambda b, c: (b, c, col_blk + 2)),
            pl.BlockSpec((None, ln, D_M), lambda b, c: (b, c, col_blk + 3)),
            pl.BlockSpec((None, ln, GATE_COLS), lambda b, c: (b, c, 0)),
            pl.BlockSpec((1, GATE_COLS), lambda b, c: (0, 0)),
            pl.BlockSpec((CONV_W, 2 * D_M), lambda b, c: (0, 0)),
            pl.BlockSpec((1, D_M), lambda b, c: (0, 0)),
        ],
        out_specs=pl.BlockSpec((None, ln, D_M), lambda b, c: (b, c, 0)),
        out_shape=jax.ShapeDtypeStruct((bsz, seq, D_M), BF16),
        scratch_shapes=[
            pltpu.VMEM((ln + SUBLANES, 2 * D_M), F32),
            pltpu.VMEM((N_HEADS_M, HEAD_DIM_M, HEAD_DIM_M), F32),
            pltpu.VMEM((N_HEADS_M, 1, HEAD_DIM_M), F32),
            pltpu.VMEM((SUBLANES, LANES), F32),
        ],
        compiler_params=pltpu.CompilerParams(
            dimension_semantics=("parallel", "arbitrary"), vmem_limit_bytes=VMEM_LIMIT_BYTES),
        name="mlstm",
    )(proj, proj, proj, proj, gates, b_gates, conv_w, norm_g)


def _sb_body(q_ref, k_ref, v_ref, o_ref):
    tq = SB_TQ
    qi = pl.program_id(2)
    q = q_ref[...]
    scale = HEAD_DIM_SB ** -0.5
    row = lax.broadcasted_iota(jnp.int32, (tq, tq), 0)
    col = lax.broadcasted_iota(jnp.int32, (tq, tq), 1)
    strict = col < row
    suffix = jnp.where(row >= col, 1.0, 0.0).astype(BF16)

    def block(kb, carry, acc, on_diagonal):
        off = pl.multiple_of(kb * tq, tq)
        z = _dot_nt(q, k_ref[pl.ds(off, tq), :]) * scale
        lom = -(jnp.maximum(z, 0.0) + jnp.log(1.0 + jnp.exp(-jnp.abs(z))))
        if on_diagonal:
            lom = jnp.where(strict, lom, 0.0)
        hi, lo = _split_bf16(lom)
        rem = _dot(hi, suffix) + _dot(lo, suffix)
        att = jnp.exp(z + (rem + carry))
        if on_diagonal:
            att = jnp.where(strict, att, 0.0)
        acc = acc + _dot(att.astype(BF16), v_ref[pl.ds(off, tq), :])
        return carry + rem[:, 0:1], acc

    carry, acc = block(qi, jnp.zeros((tq, 1), F32), jnp.zeros((tq, HEAD_DIM_SB), F32), True)

    def step(j, state):
        return block(qi - 1 - j, state[0], state[1], False)

    carry, acc = lax.fori_loop(0, qi, step, (carry, acc))
    o_ref[...] = acc.astype(BF16)


def _stick_breaking(proj, bsz, seq):
    tq = SB_TQ
    q_blk = (2 * D_MODEL + 4 * D_M) // HEAD_DIM_SB
    k_blk = q_blk + N_HEADS_SB
    v_blk = k_blk + N_HEADS_SB
    return pl.pallas_call(
        _sb_body,
        grid=(bsz, N_HEADS_SB, seq // tq),
        in_specs=[
            pl.BlockSpec((None, tq, HEAD_DIM_SB), lambda b, h, i: (b, i, q_blk + h)),
            pl.BlockSpec((None, seq, HEAD_DIM_SB), lambda b, h, i: (b, 0, k_blk + h)),
            pl.BlockSpec((None, seq, HEAD_DIM_SB), lambda b, h, i: (b, 0, v_blk + h)),
        ],
        out_specs=pl.BlockSpec((None, tq, HEAD_DIM_SB), lambda b, h, i: (b, i, h)),
        out_shape=jax.ShapeDtypeStruct((bsz, seq, D_SB), BF16),
        compiler_params=pltpu.CompilerParams(
            dimension_semantics=("parallel", "parallel", "arbitrary"), vmem_limit_bytes=VMEM_LIMIT_BYTES),
        name="stickbreak",
    )(proj, proj, proj)


def _merge_ln_body(ym_ref, ysb_ref, ga_ref, gb_ref, x_ref, wum_ref, wus_ref, wo_ref, g_ref, b_ref, o_ref):
    um = _dot(ym_ref[...], wum_ref[...])
    us = _dot(ysb_ref[...], wus_ref[...])
    merged = _sigmoid(ga_ref[...].astype(F32)) * um + _sigmoid(gb_ref[...].astype(F32)) * us
    mix = _dot(merged.astype(BF16), wo_ref[...])
    o_ref[...] = _layer_norm(ALPHA * x_ref[...] + mix, g_ref[...], b_ref[...])


def _merge_ln(ym, ysb, proj, x, w_up_m, w_up_sb, w_out, g, b):
    t = x.shape[0]
    tm = MERGE_TM
    const = dict(pipeline_mode=pl.Buffered(1))
    return pl.pallas_call(
        _merge_ln_body,
        grid=(t // tm,),
        in_specs=[
            pl.BlockSpec((tm, D_M), lambda i: (i, 0)),
            pl.BlockSpec((tm, D_SB), lambda i: (i, 0)),
            pl.BlockSpec((tm, D_MODEL), lambda i: (i, 0)),
            pl.BlockSpec((tm, D_MODEL), lambda i: (i, 1)),
            pl.BlockSpec((tm, D_MODEL), lambda i: (i, 0)),
            pl.BlockSpec((D_M, D_MODEL), lambda i: (0, 0), **const),
            pl.BlockSpec((D_SB, D_MODEL), lambda i: (0, 0), **const),
            pl.BlockSpec((D_MODEL, D_MODEL), lambda i: (0, 0), **const),
            pl.BlockSpec((1, D_MODEL), lambda i: (0, 0)),
            pl.BlockSpec((1, D_MODEL), lambda i: (0, 0)),
        ],
        out_specs=pl.BlockSpec((tm, D_MODEL), lambda i: (i, 0)),
        out_shape=jax.ShapeDtypeStruct((t, D_MODEL), F32),
        compiler_params=pltpu.CompilerParams(
            dimension_semantics=("parallel",), vmem_limit_bytes=VMEM_LIMIT_BYTES),
        name="merge_ln",
    )(ym, ysb, proj, proj, x, w_up_m, w_up_sb, w_out, g, b)


def _ple_body(x_ref, p_ref, wg_ref, wp_ref, o_ref):
    x = x_ref[...]
    gate = _sigmoid(_dot(x.astype(BF16), wg_ref[...]))
    o_ref[...] = x + gate * _dot(p_ref[...].astype(BF16), wp_ref[...])


def _ple(x, p, w_gate, w_proj):
    t = x.shape[0]
    tm = PLE_TM
    const = dict(pipeline_mode=pl.Buffered(1))
    return pl.pallas_call(
        _ple_body,
        grid=(t // tm,),
        in_specs=[
            pl.BlockSpec((tm, D_MODEL), lambda i: (i, 0)),
            pl.BlockSpec((tm, D_PLE), lambda i: (i, 0)),
            pl.BlockSpec((D_MODEL, D_MODEL), lambda i: (0, 0), **const),
            pl.BlockSpec((D_PLE, D_MODEL), lambda i: (0, 0), **const),
        ],
        out_specs=pl.BlockSpec((tm, D_MODEL), lambda i: (i, 0)),
        out_shape=jax.ShapeDtypeStruct((t, D_MODEL), F32),
        compiler_params=pltpu.CompilerParams(
            dimension_semantics=("parallel",), vmem_limit_bytes=VMEM_LIMIT_BYTES),
        name="ple",
    )(x, p, w_gate, w_proj)


def _split_w_in(w_in):
    sizes = [D_M, D_M, D_M, D_M, N_HEADS_M, N_HEADS_M, D_SB, D_SB, D_SB, D_MODEL, D_MODEL]
    parts, off = [], 0
    for s in sizes:
        parts.append(w_in[:, off:off + s])
        off += s
    return parts


def kernel(x, p, ffn1_w1, ffn1_w3, ffn1_w2, ln1_g, ln1_b, w_in, b_gates_m, conv_m, norm_m, w_up_m, w_up_sb, w_out, ln2_g, ln2_b, ffn2_w1, ffn2_w3, ffn2_w2, ln3_g, ln3_b, w_ple_gate, w_ple_proj):
    bsz, seq, _ = x.shape
    t = bsz * seq
    xf = x.reshape(t, D_MODEL)
    row = lambda v: v.reshape(1, -1)
    for i in range(DEPTH):
        xf = _ffn_ln(xf, ffn1_w1[i].astype(BF16), ffn1_w3[i].astype(BF16), ffn1_w2[i].astype(BF16),
                     row(ln1_g[i]), row(ln1_b[i]))

        mq, mk, mv, mo, mi, mf, sq, sk, sv, ga, gb = _split_w_in(w_in[i])
        w_main = jnp.concatenate([ga, gb, mq, mk, mv, mo, sq, sk, sv], axis=1).astype(BF16)
        gate_pad = jnp.zeros((D_MODEL, GATE_COLS - 2 * N_HEADS_M), w_in.dtype)
        w_gates = jnp.concatenate([mi, mf, gate_pad], axis=1).astype(BF16)
        b_gates = jnp.concatenate([b_gates_m[i], jnp.zeros((GATE_COLS - 2 * N_HEADS_M,), F32)]).reshape(1, GATE_COLS)
        proj, gates = _in_proj(xf, w_main, w_gates)

        proj3 = proj.reshape(bsz, seq, PROJ_COLS)
        ym = _mlstm(proj3, gates.reshape(bsz, seq, GATE_COLS), b_gates, conv_m[i], row(norm_m[i]), bsz, seq)
        ysb = _stick_breaking(proj3, bsz, seq)

        xf = _merge_ln(ym.reshape(t, D_M), ysb.reshape(t, D_SB), proj, xf,
                       w_up_m[i].astype(BF16), w_up_sb[i].astype(BF16), w_out[i].astype(BF16),
                       row(ln2_g[i]), row(ln2_b[i]))
        xf = _ffn_ln(xf, ffn2_w1[i].astype(BF16), ffn2_w3[i].astype(BF16), ffn2_w2[i].astype(BF16),
                     row(ln3_g[i]), row(ln3_b[i]))
        xf = _ple(xf, p[i].reshape(t, D_PLE), w_ple_gate[i].astype(BF16), w_ple_proj[i].astype(BF16))
    return xf.reshape(bsz, seq, D_MODEL)
```

```python
import functools

import jax
import jax.numpy as jnp
from jax import lax
from jax.experimental import pallas as pl
from jax.experimental.pallas import tpu as pltpu

D_MODEL = 2048
DEPTH = 1
N_HEADS_M = 4
HEAD_DIM_M = 256
D_M = N_HEADS_M * HEAD_DIM_M
N_HEADS_SB = 8
HEAD_DIM_SB = 128
D_SB = N_HEADS_SB * HEAD_DIM_SB
CONV_W = 4
D_FF = 5632
D_PLE = 256
ALPHA = (2.0 * DEPTH) ** 0.25
LN_EPS = 1e-5
NEG_BIG = -1e30

LANES = 128
SUBLANES = 8
VMEM_LIMIT_BYTES = 56 * 1024 * 1024

PROJ_COLS = 2 * D_MODEL + 4 * D_M + 3 * D_SB
GATE_COLS = LANES

FFN_TM = 512
FFN_TF = 512
PROJ_TM = 1024
PROJ_TN = 1024
MLSTM_CHUNK = 256
SB_TQ = 256
SB_HEADS_PER_STEP = 2
LOG2_E = 1.4426950408889634
SB_SKIP_LOG2 = -160.0
MERGE_TM = 256
PLE_TM = 512

F32 = jnp.float32
BF16 = jnp.bfloat16


def _sigmoid(x):
    return 1.0 / (1.0 + jnp.exp(-x))


def _log_sigmoid(x):
    return jnp.minimum(x, 0.0) - jnp.log(1.0 + jnp.exp(-jnp.abs(x)))


def _layer_norm(y, g, b):
    mu = jnp.mean(y, axis=-1, keepdims=True)
    yc = y - mu
    var = jnp.mean(yc * yc, axis=-1, keepdims=True)
    return yc * lax.rsqrt(var + LN_EPS) * g + b


def _split_bf16(x):
    hi = x.astype(BF16)
    lo = (x - hi.astype(F32)).astype(BF16)
    return hi, lo


def _dot(a, b):
    return jnp.dot(a, b, preferred_element_type=F32)


def _dot_nt(a, b):
    return lax.dot_general(a, b, (((1,), (1,)), ((), ())), preferred_element_type=F32)


def _dot_tn(a, b):
    return lax.dot_general(a, b, (((0,), (0,)), ((), ())), preferred_element_type=F32)


def _ffn_ln_body(x_ref, w1_ref, w3_ref, w2_ref, g_ref, b_ref, o_ref, xb_ref, acc_ref):
    k = pl.program_id(1)

    @pl.when(k == 0)
    def _():
        xb_ref[...] = x_ref[...].astype(BF16)
        acc_ref[...] = jnp.zeros_like(acc_ref)

    xb = xb_ref[...]
    h1 = _dot(xb, w1_ref[...])
    h3 = _dot(xb, w3_ref[...])
    h = (h1 * _sigmoid(h1)) * h3
    acc_ref[...] += _dot(h.astype(BF16), w2_ref[...])

    @pl.when(k == pl.num_programs(1) - 1)
    def _():
        y = ALPHA * x_ref[...] + 0.5 * acc_ref[...]
        o_ref[...] = _layer_norm(y, g_ref[...], b_ref[...])


def _ffn_ln(x, w1, w3, w2, g, b):
    t = x.shape[0]
    grid = (t // FFN_TM, D_FF // FFN_TF)
    return pl.pallas_call(
        _ffn_ln_body,
        grid=grid,
        in_specs=[
            pl.BlockSpec((FFN_TM, D_MODEL), lambda i, k: (i, 0)),
            pl.BlockSpec((D_MODEL, FFN_TF), lambda i, k: (0, k)),
            pl.BlockSpec((D_MODEL, FFN_TF), lambda i, k: (0, k)),
            pl.BlockSpec((FFN_TF, D_MODEL), lambda i, k: (k, 0)),
            pl.BlockSpec((1, D_MODEL), lambda i, k: (0, 0)),
            pl.BlockSpec((1, D_MODEL), lambda i, k: (0, 0)),
        ],
        out_specs=pl.BlockSpec((FFN_TM, D_MODEL), lambda i, k: (i, 0)),
        out_shape=jax.ShapeDtypeStruct((t, D_MODEL), F32),
        scratch_shapes=[pltpu.VMEM((FFN_TM, D_MODEL), BF16), pltpu.VMEM((FFN_TM, D_MODEL), F32)],
        compiler_params=pltpu.CompilerParams(
            dimension_semantics=("parallel", "arbitrary"), vmem_limit_bytes=VMEM_LIMIT_BYTES),
        name="ffn_ln",
    )(x, w1, w3, w2, g, b)


def _in_proj_body(x_ref, w_ref, wg_ref, o_ref, g_ref, xb_ref):
    j = pl.program_id(1)

    @pl.when(j == 0)
    def _():
        xb = x_ref[...].astype(BF16)
        xb_ref[...] = xb
        g_ref[...] = _dot(xb, wg_ref[...])

    o_ref[...] = _dot(xb_ref[...], w_ref[...]).astype(BF16)


def _in_proj(x, w_main, w_gates):
    t = x.shape[0]
    grid = (t // PROJ_TM, PROJ_COLS // PROJ_TN)
    return pl.pallas_call(
        _in_proj_body,
        grid=grid,
        in_specs=[
            pl.BlockSpec((PROJ_TM, D_MODEL), lambda i, j: (i, 0)),
            pl.BlockSpec((D_MODEL, PROJ_TN), lambda i, j: (0, j)),
            pl.BlockSpec((D_MODEL, GATE_COLS), lambda i, j: (0, 0)),
        ],
        out_specs=[
            pl.BlockSpec((PROJ_TM, PROJ_TN), lambda i, j: (i, j)),
            pl.BlockSpec((PROJ_TM, GATE_COLS), lambda i, j: (i, 0)),
        ],
        out_shape=[
            jax.ShapeDtypeStruct((t, PROJ_COLS), BF16),
            jax.ShapeDtypeStruct((t, GATE_COLS), F32),
        ],
        scratch_shapes=[pltpu.VMEM((PROJ_TM, D_MODEL), BF16)],
        compiler_params=pltpu.CompilerParams(
            dimension_semantics=("parallel", "arbitrary"), vmem_limit_bytes=VMEM_LIMIT_BYTES),
        name="in_proj",
    )(x, w_main, w_gates)


def _mlstm_body(q_ref, k_ref, v_ref, og_ref, g_ref, bg_ref, conv_ref, norm_ref, y_ref,
                cbuf_ref, c_ref, n_ref, m_ref):
    ln = MLSTM_CHUNK
    dh = HEAD_DIM_M

    @pl.when(pl.program_id(1) == 0)
    def _():
        cbuf_ref[0:SUBLANES, :] = jnp.zeros((SUBLANES, 2 * D_M), F32)
        c_ref[...] = jnp.zeros_like(c_ref)
        n_ref[...] = jnp.zeros_like(n_ref)
        m_ref[...] = jnp.full_like(m_ref, NEG_BIG)

    cbuf_ref[SUBLANES:SUBLANES + ln, 0:D_M] = q_ref[...].astype(F32)
    cbuf_ref[SUBLANES:SUBLANES + ln, D_M:2 * D_M] = k_ref[...].astype(F32)

    def conv_silu(col0):
        base = SUBLANES - (CONV_W - 1)
        acc = cbuf_ref[base:base + ln, col0:col0 + dh] * conv_ref[0:1, col0:col0 + dh]
        for j in range(1, CONV_W):
            acc = acc + cbuf_ref[base + j:base + j + ln, col0:col0 + dh] * conv_ref[j:j + 1, col0:col0 + dh]
        return acc * _sigmoid(acc)

    gall = g_ref[...] + bg_ref[...]
    lane = lax.broadcasted_iota(jnp.int32, (ln, GATE_COLS), 1)
    is_f = (lane >= N_HEADS_M) & (lane < 2 * N_HEADS_M)
    lf = jnp.where(is_f, _log_sigmoid(gall), 0.0)
    row = lax.broadcasted_iota(jnp.int32, (ln, ln), 0)
    col = lax.broadcasted_iota(jnp.int32, (ln, ln), 1)
    causal = col <= row
    tri = jnp.where(causal, 1.0, 0.0).astype(BF16)
    lf_hi, lf_lo = _split_bf16(lf)
    b_all = _dot(tri, lf_hi) + _dot(tri, lf_lo)
    b_all_t = b_all.T
    gall_t = gall.T

    for h in range(N_HEADS_M):
        hs = slice(h * dh, (h + 1) * dh)
        qh = conv_silu(h * dh)
        kh = conv_silu(D_M + h * dh) * (HEAD_DIM_M ** -0.5)
        qb = qh.astype(BF16)
        kb = kh.astype(BF16)
        vb = v_ref[:, hs]

        b_col = b_all[:, N_HEADS_M + h:N_HEADS_M + h + 1]
        li_col = gall[:, h:h + 1]
        b_row = b_all_t[N_HEADS_M + h:N_HEADS_M + h + 1, :]
        li_row = gall_t[h:h + 1, :]
        m_prev = m_ref[h:h + 1, 0:1]

        d_log = jnp.where(causal, (b_col - b_row) + li_row, NEG_BIG)
        inter_log = b_col + m_prev
        m_t = jnp.maximum(jnp.max(d_log, axis=-1, keepdims=True), inter_log)
        scores = _dot_nt(qb, kb) * jnp.exp(d_log - m_t)
        inter_scale = jnp.exp(inter_log - m_t)
        c_prev = c_ref[h]
        n_prev = n_ref[h]
        num = _dot(scores.astype(BF16), vb) + inter_scale * _dot(qb, c_prev.astype(BF16))
        den = (jnp.sum(scores, axis=-1, keepdims=True)
               + inter_scale * jnp.sum(qh * n_prev, axis=-1, keepdims=True))
        hval = num / jnp.maximum(jnp.abs(den), jnp.exp(-m_t))

        g_last = b_col[ln - 1:ln, :]
        w_log = (g_last - b_col) + li_col
        m_new = jnp.maximum(g_last + m_prev, jnp.max(w_log, axis=0, keepdims=True))
        decay = jnp.exp((g_last + m_prev) - m_new)
        kw = kh * jnp.exp(w_log - m_new)
        c_ref[h] = decay * c_prev + _dot_tn(kw.astype(BF16), vb)
        n_ref[h] = decay * n_prev + jnp.sum(kw, axis=0, keepdims=True)
        m_ref[h:h + 1, :] = jnp.broadcast_to(m_new, (1, LANES))

        mu = jnp.mean(hval, axis=-1, keepdims=True)
        hc = hval - mu
        var = jnp.mean(hc * hc, axis=-1, keepdims=True)
        hn = hc * lax.rsqrt(var + LN_EPS) * norm_ref[:, hs]
        y_ref[:, hs] = (hn * _sigmoid(og_ref[:, hs].astype(F32))).astype(BF16)

    cbuf_ref[0:SUBLANES, :] = cbuf_ref[ln:ln + SUBLANES, :]


def _mlstm(proj, gates, b_gates, conv_w, norm_g, bsz, seq):
    ln = MLSTM_CHUNK
    n_chunks = seq // ln
    col_blk = 2 * D_MODEL // D_M
    return pl.pallas_call(
        _mlstm_body,
        grid=(bsz, n_chunks),
        in_specs=[
            pl.BlockSpec((None, ln, D_M), lambda b, c: (b, c, col_blk)),
            pl.BlockSpec((None, ln, D_M), lambda b, c: (b, c, col_blk + 1)),
            pl.BlockSpec((None, ln, D_M), lambda b, c: (b, c, col_blk + 2)),
            pl.BlockSpec((None, ln, D_M), lambda b, c: (b, c, col_blk + 3)),
            pl.BlockSpec((None, ln, GATE_COLS), lambda b, c: (b, c, 0)),
            pl.BlockSpec((1, GATE_COLS), lambda b, c: (0, 0)),
            pl.BlockSpec((CONV_W, 2 * D_M), lambda b, c: (0, 0)),
            pl.BlockSpec((1, D_M), lambda b, c: (0, 0)),
        ],
        out_specs=pl.BlockSpec((None, ln, D_M), lambda b, c: (b, c, 0)),
        out_shape=jax.ShapeDtypeStruct((bsz, seq, D_M), BF16),
        scratch_shapes=[
            pltpu.VMEM((ln + SUBLANES, 2 * D_M), F32),
            pltpu.VMEM((N_HEADS_M, HEAD_DIM_M, HEAD_DIM_M), F32),
            pltpu.VMEM((N_HEADS_M, 1, HEAD_DIM_M), F32),
            pltpu.VMEM((SUBLANES, LANES), F32),
        ],
        compiler_params=pltpu.CompilerParams(
            dimension_semantics=("parallel", "arbitrary"), vmem_limit_bytes=VMEM_LIMIT_BYTES),
        name="mlstm",
    )(proj, proj, proj, proj, gates, b_gates, conv_w, norm_g)


def _sb_body(q_ref, k_ref, v_ref, o_ref):
    tq = SB_TQ
    dh = HEAD_DIM_SB
    qi = pl.program_id(2)
    z_scale = (HEAD_DIM_SB ** -0.5) * LOG2_E
    row = lax.broadcasted_iota(jnp.int32, (tq, tq), 0)
    col = lax.broadcasted_iota(jnp.int32, (tq, tq), 1)
    strict = col < row
    row2 = lax.broadcasted_iota(jnp.int32, (2 * tq, tq), 0)
    col2 = lax.broadcasted_iota(jnp.int32, (2 * tq, tq), 1)
    suffix2 = jnp.where((row2 & (tq - 1)) >= col2, 1.0, 0.0).astype(BF16)
    heads = range(SB_HEADS_PER_STEP)
    qs = [q_ref[:, hh * dh:(hh + 1) * dh] for hh in heads]

    def block(kb, state, on_diagonal):
        off = pl.multiple_of(kb * tq, tq)
        out = []
        for hh in heads:
            carry, acc = state[2 * hh], state[2 * hh + 1]
            hs = slice(hh * dh, (hh + 1) * dh)
            z2 = _dot_nt(qs[hh], k_ref[pl.ds(off, tq), hs]) * z_scale
            nz2 = -z2
            lom2 = jnp.minimum(nz2, 0.0) - jnp.log2(1.0 + jnp.exp2(jnp.minimum(z2, nz2)))
            if on_diagonal:
                lom2 = jnp.where(strict, lom2, 0.0)
            hi, lo = _split_bf16(lom2)
            rem = _dot(jnp.concatenate([hi, lo], axis=1), suffix2)
            att = jnp.exp2(z2 + (rem + carry))
            if on_diagonal:
                att = jnp.where(strict, att, 0.0)
            acc = acc + _dot(att.astype(BF16), v_ref[pl.ds(off, tq), hs])
            out += [carry + rem[:, 0:1], acc]
        return tuple(out)

    def carry_max(state):
        cm = state[0]
        for hh in heads[1:]:
            cm = jnp.maximum(cm, state[2 * hh])
        return jnp.max(cm)

    state = ()
    for hh in heads:
        state += (jnp.zeros((tq, 1), F32), jnp.zeros((tq, dh), F32))
    state = block(qi, state, True)

    def cond(loop):
        j, cmax = loop[0], loop[1]
        return jnp.logical_and(j < qi, cmax > SB_SKIP_LOG2)

    def step(loop):
        j = loop[0]
        new = block(qi - 1 - j, loop[2:], False)
        return (j + 1, carry_max(new)) + new

    final = lax.while_loop(cond, step, (jnp.int32(0), carry_max(state)) + state)[2:]
    for hh in heads:
        o_ref[:, hh * dh:(hh + 1) * dh] = final[2 * hh + 1].astype(BF16)


def _stick_breaking(proj, bsz, seq):
    tq = SB_TQ
    width = SB_HEADS_PER_STEP * HEAD_DIM_SB
    q_blk = (2 * D_MODEL + 4 * D_M) // width
    k_blk = q_blk + D_SB // width
    v_blk = k_blk + D_SB // width
    return pl.pallas_call(
        _sb_body,
        grid=(bsz, D_SB // width, seq // tq),
        in_specs=[
            pl.BlockSpec((None, tq, width), lambda b, h, i: (b, i, q_blk + h)),
            pl.BlockSpec((None, seq, width), lambda b, h, i: (b, 0, k_blk + h)),
            pl.BlockSpec((None, seq, width), lambda b, h, i: (b, 0, v_blk + h)),
        ],
        out_specs=pl.BlockSpec((None, tq, width), lambda b, h, i: (b, i, h)),
        out_shape=jax.ShapeDtypeStruct((bsz, seq, D_SB), BF16),
        compiler_params=pltpu.CompilerParams(
            dimension_semantics=("parallel", "parallel", "arbitrary"), vmem_limit_bytes=VMEM_LIMIT_BYTES),
        name="stickbreak",
    )(proj, proj, proj)


def _merge_ln_body(ym_ref, ysb_ref, ga_ref, gb_ref, x_ref, wum_ref, wus_ref, wo_ref, g_ref, b_ref, o_ref):
    um = _dot(ym_ref[...], wum_ref[...])
    us = _dot(ysb_ref[...], wus_ref[...])
    merged = _sigmoid(ga_ref[...].astype(F32)) * um + _sigmoid(gb_ref[...].astype(F32)) * us
    mix = _dot(merged.astype(BF16), wo_ref[...])
    o_ref[...] = _layer_norm(ALPHA * x_ref[...] + mix, g_ref[...], b_ref[...])


def _merge_ln(ym, ysb, proj, x, w_up_m, w_up_sb, w_out, g, b):
    t = x.shape[0]
    tm = MERGE_TM
    const = dict(pipeline_mode=pl.Buffered(1))
    return pl.pallas_call(
        _merge_ln_body,
        grid=(t // tm,),
        in_specs=[
            pl.BlockSpec((tm, D_M), lambda i: (i, 0)),
            pl.BlockSpec((tm, D_SB), lambda i: (i, 0)),
            pl.BlockSpec((tm, D_MODEL), lambda i: (i, 0)),
            pl.BlockSpec((tm, D_MODEL), lambda i: (i, 1)),
            pl.BlockSpec((tm, D_MODEL), lambda i: (i, 0)),
            pl.BlockSpec((D_M, D_MODEL), lambda i: (0, 0), **const),
            pl.BlockSpec((D_SB, D_MODEL), lambda i: (0, 0), **const),
            pl.BlockSpec((D_MODEL, D_MODEL), lambda i: (0, 0), **const),
            pl.BlockSpec((1, D_MODEL), lambda i: (0, 0)),
            pl.BlockSpec((1, D_MODEL), lambda i: (0, 0)),
        ],
        out_specs=pl.BlockSpec((tm, D_MODEL), lambda i: (i, 0)),
        out_shape=jax.ShapeDtypeStruct((t, D_MODEL), F32),
        compiler_params=pltpu.CompilerParams(
            dimension_semantics=("parallel",), vmem_limit_bytes=VMEM_LIMIT_BYTES),
        name="merge_ln",
    )(ym, ysb, proj, proj, x, w_up_m, w_up_sb, w_out, g, b)


def _ple_body(x_ref, p_ref, wg_ref, wp_ref, o_ref):
    x = x_ref[...]
    gate = _sigmoid(_dot(x.astype(BF16), wg_ref[...]))
    o_ref[...] = x + gate * _dot(p_ref[...].astype(BF16), wp_ref[...])


def _ple(x, p, w_gate, w_proj):
    t = x.shape[0]
    tm = PLE_TM
    const = dict(pipeline_mode=pl.Buffered(1))
    return pl.pallas_call(
        _ple_body,
        grid=(t // tm,),
        in_specs=[
            pl.BlockSpec((tm, D_MODEL), lambda i: (i, 0)),
            pl.BlockSpec((tm, D_PLE), lambda i: (i, 0)),
            pl.BlockSpec((D_MODEL, D_MODEL), lambda i: (0, 0), **const),
            pl.BlockSpec((D_PLE, D_MODEL), lambda i: (0, 0), **const),
        ],
        out_specs=pl.BlockSpec((tm, D_MODEL), lambda i: (i, 0)),
        out_shape=jax.ShapeDtypeStruct((t, D_MODEL), F32),
        compiler_params=pltpu.CompilerParams(
            dimension_semantics=("parallel",), vmem_limit_bytes=VMEM_LIMIT_BYTES),
        name="ple",
    )(x, p, w_gate, w_proj)


def _split_w_in(w_in):
    sizes = [D_M, D_M, D_M, D_M, N_HEADS_M, N_HEADS_M, D_SB, D_SB, D_SB, D_MODEL, D_MODEL]
    parts, off = [], 0
    for s in sizes:
        parts.append(w_in[:, off:off + s])
        off += s
    return parts


def kernel(x, p, ffn1_w1, ffn1_w3, ffn1_w2, ln1_g, ln1_b, w_in, b_gates_m, conv_m, norm_m, w_up_m, w_up_sb, w_out, ln2_g, ln2_b, ffn2_w1, ffn2_w3, ffn2_w2, ln3_g, ln3_b, w_ple_gate, w_ple_proj):
    bsz, seq, _ = x.shape
    t = bsz * seq
    xf = x.reshape(t, D_MODEL)
    row = lambda v: v.reshape(1, -1)
    for i in range(DEPTH):
        xf = _ffn_ln(xf, ffn1_w1[i].astype(BF16), ffn1_w3[i].astype(BF16), ffn1_w2[i].astype(BF16),
                     row(ln1_g[i]), row(ln1_b[i]))

        mq, mk, mv, mo, mi, mf, sq, sk, sv, ga, gb = _split_w_in(w_in[i])
        w_main = jnp.concatenate([ga, gb, mq, mk, mv, mo, sq, sk, sv], axis=1).astype(BF16)
        gate_pad = jnp.zeros((D_MODEL, GATE_COLS - 2 * N_HEADS_M), w_in.dtype)
        w_gates = jnp.concatenate([mi, mf, gate_pad], axis=1).astype(BF16)
        b_gates = jnp.concatenate([b_gates_m[i], jnp.zeros((GATE_COLS - 2 * N_HEADS_M,), F32)]).reshape(1, GATE_COLS)
        proj, gates = _in_proj(xf, w_main, w_gates)

        proj3 = proj.reshape(bsz, seq, PROJ_COLS)
        ym = _mlstm(proj3, gates.reshape(bsz, seq, GATE_COLS), b_gates, conv_m[i], row(norm_m[i]), bsz, seq)
        ysb = _stick_breaking(proj3, bsz, seq)

        xf = _merge_ln(ym.reshape(t, D_M), ysb.reshape(t, D_SB), proj, xf,
                       w_up_m[i].astype(BF16), w_up_sb[i].astype(BF16), w_out[i].astype(BF16),
                       row(ln2_g[i]), row(ln2_b[i]))
        xf = _ffn_ln(xf, ffn2_w1[i].astype(BF16), ffn2_w3[i].astype(BF16), ffn2_w2[i].astype(BF16),
                     row(ln3_g[i]), row(ln3_b[i]))
        xf = _ple(xf, p[i].reshape(t, D_PLE), w_ple_gate[i].astype(BF16), w_ple_proj[i].astype(BF16))
    return xf.reshape(bsz, seq, D_MODEL)
```

```python
import functools

import jax
import jax.numpy as jnp
from jax import lax
from jax.experimental import pallas as pl
from jax.experimental.pallas import tpu as pltpu

D_MODEL = 2048
DEPTH = 1
N_HEADS_M = 4
HEAD_DIM_M = 256
D_M = N_HEADS_M * HEAD_DIM_M
N_HEADS_SB = 8
HEAD_DIM_SB = 128
D_SB = N_HEADS_SB * HEAD_DIM_SB
CONV_W = 4
D_FF = 5632
D_PLE = 256
ALPHA = (2.0 * DEPTH) ** 0.25
LN_EPS = 1e-5
NEG_BIG = -1e30

LANES = 128
SUBLANES = 8
VMEM_LIMIT_BYTES = 56 * 1024 * 1024

PROJ_COLS = 2 * D_MODEL + 4 * D_M + 3 * D_SB
GATE_COLS = LANES

FFN_TM = 512
FFN_TF = 512
PROJ_TM = 2048
PROJ_TN = 1024
MLSTM_CHUNK = 256
SB_TQ = 256
SB_HEADS_PER_STEP = 4
LOG2_E = 1.4426950408889634
SB_SKIP_LOG2 = -160.0
MERGE_TM = 256
PLE_TM = 512

F32 = jnp.float32
BF16 = jnp.bfloat16


def _sigmoid(x):
    return 1.0 / (1.0 + jnp.exp(-x))


def _log_sigmoid(x):
    return jnp.minimum(x, 0.0) - jnp.log(1.0 + jnp.exp(-jnp.abs(x)))


def _layer_norm(y, g, b):
    mu = jnp.mean(y, axis=-1, keepdims=True)
    yc = y - mu
    var = jnp.mean(yc * yc, axis=-1, keepdims=True)
    return yc * lax.rsqrt(var + LN_EPS) * g + b


def _split_bf16(x):
    hi = x.astype(BF16)
    lo = (x - hi.astype(F32)).astype(BF16)
    return hi, lo


def _dot(a, b):
    return jnp.dot(a, b, preferred_element_type=F32)


def _dot_nt(a, b):
    return lax.dot_general(a, b, (((1,), (1,)), ((), ())), preferred_element_type=F32)


def _dot_tn(a, b):
    return lax.dot_general(a, b, (((0,), (0,)), ((), ())), preferred_element_type=F32)


def _ffn_ln_body(x_ref, w1_ref, w3_ref, w2_ref, g_ref, b_ref, o_ref, *rest):
    maybe_ob_ref, (xb_ref, acc_ref) = rest[:-2], rest[-2:]
    k = pl.program_id(1)

    @pl.when(k == 0)
    def _():
        xb_ref[...] = x_ref[...].astype(BF16)
        acc_ref[...] = jnp.zeros_like(acc_ref)

    xb = xb_ref[...]
    h1 = _dot(xb, w1_ref[...])
    h3 = _dot(xb, w3_ref[...])
    h = (h1 * _sigmoid(h1)) * h3
    acc_ref[...] += _dot(h.astype(BF16), w2_ref[...])

    @pl.when(k == pl.num_programs(1) - 1)
    def _():
        y = ALPHA * x_ref[...] + 0.5 * acc_ref[...]
        out = _layer_norm(y, g_ref[...], b_ref[...])
        o_ref[...] = out
        for ob_ref in maybe_ob_ref:
            ob_ref[...] = out.astype(BF16)


def _ffn_ln(x, w1, w3, w2, g, b, also_bf16):
    t = x.shape[0]
    grid = (t // FFN_TM, D_FF // FFN_TF)
    n_out = 2 if also_bf16 else 1
    out_dtypes = (F32, BF16)[:n_out]
    return pl.pallas_call(
        _ffn_ln_body,
        grid=grid,
        in_specs=[
            pl.BlockSpec((FFN_TM, D_MODEL), lambda i, k: (i, 0)),
            pl.BlockSpec((D_MODEL, FFN_TF), lambda i, k: (0, k)),
            pl.BlockSpec((D_MODEL, FFN_TF), lambda i, k: (0, k)),
            pl.BlockSpec((FFN_TF, D_MODEL), lambda i, k: (k, 0)),
            pl.BlockSpec((1, D_MODEL), lambda i, k: (0, 0)),
            pl.BlockSpec((1, D_MODEL), lambda i, k: (0, 0)),
        ],
        out_specs=[pl.BlockSpec((FFN_TM, D_MODEL), lambda i, k: (i, 0)) for _ in out_dtypes],
        out_shape=[jax.ShapeDtypeStruct((t, D_MODEL), dt) for dt in out_dtypes],
        scratch_shapes=[pltpu.VMEM((FFN_TM, D_MODEL), BF16), pltpu.VMEM((FFN_TM, D_MODEL), F32)],
        compiler_params=pltpu.CompilerParams(
            dimension_semantics=("parallel", "arbitrary"), vmem_limit_bytes=VMEM_LIMIT_BYTES),
        name="ffn_ln",
    )(x, w1, w3, w2, g, b)


def _in_proj_body(x_ref, w_ref, wg_ref, o_ref, g_ref):
    @pl.when(pl.program_id(1) == 0)
    def _():
        g_ref[...] = _dot(x_ref[...], wg_ref[...])

    o_ref[...] = _dot(x_ref[...], w_ref[...]).astype(BF16)


def _in_proj(x, w_main, w_gates):
    t = x.shape[0]
    grid = (t // PROJ_TM, PROJ_COLS // PROJ_TN)
    return pl.pallas_call(
        _in_proj_body,
        grid=grid,
        in_specs=[
            pl.BlockSpec((PROJ_TM, D_MODEL), lambda i, j: (i, 0)),
            pl.BlockSpec((D_MODEL, PROJ_TN), lambda i, j: (0, j)),
            pl.BlockSpec((D_MODEL, GATE_COLS), lambda i, j: (0, 0)),
        ],
        out_specs=[
            pl.BlockSpec((PROJ_TM, PROJ_TN), lambda i, j: (i, j)),
            pl.BlockSpec((PROJ_TM, GATE_COLS), lambda i, j: (i, 0)),
        ],
        out_shape=[
            jax.ShapeDtypeStruct((t, PROJ_COLS), BF16),
            jax.ShapeDtypeStruct((t, GATE_COLS), F32),
        ],
        compiler_params=pltpu.CompilerParams(
            dimension_semantics=("parallel", "arbitrary"), vmem_limit_bytes=VMEM_LIMIT_BYTES),
        name="in_proj",
    )(x, w_main, w_gates)


def _mlstm_body(q_ref, k_ref, v_ref, og_ref, g_ref, bg_ref, conv_ref, norm_ref, y_ref,
                tail_ref, shift_ref, qk_ref, c_ref, n_ref, m_ref):
    ln = MLSTM_CHUNK
    dh = HEAD_DIM_M
    taps = CONV_W - 1

    @pl.when(pl.program_id(1) == 0)
    def _():
        tail_ref[0:SUBLANES, :] = jnp.zeros((SUBLANES, 2 * D_M), F32)
        c_ref[...] = jnp.zeros_like(c_ref)
        n_ref[...] = jnp.zeros_like(n_ref)
        m_ref[...] = jnp.full_like(m_ref, NEG_BIG)
        r = lax.broadcasted_iota(jnp.int32, (taps * ln, ln), 0)
        s = lax.broadcasted_iota(jnp.int32, (taps * ln, ln), 1)
        shift_ref[...] = jnp.where(s == (r & (ln - 1)) - (r // ln + 1), 1.0, 0.0).astype(BF16)

    for part, x_ref in enumerate((q_ref, k_ref)):
        for h in range(N_HEADS_M):
            cs = slice(part * D_M + h * dh, part * D_M + (h + 1) * dh)
            x = x_ref[:, h * dh:(h + 1) * dh]
            sh = _dot(shift_ref[...], x)
            acc = sh[(taps - 1) * ln:taps * ln] * conv_ref[0:1, cs]
            for j in range(1, taps):
                acc = acc + sh[(taps - 1 - j) * ln:(taps - j) * ln] * conv_ref[j:j + 1, cs]
            qk_ref[:, cs] = acc + x.astype(F32) * conv_ref[taps:taps + 1, cs]
    tail_ref[SUBLANES:2 * SUBLANES, 0:D_M] = q_ref[0:2 * SUBLANES, :].astype(F32)[0:SUBLANES]
    tail_ref[SUBLANES:2 * SUBLANES, D_M:2 * D_M] = k_ref[0:2 * SUBLANES, :].astype(F32)[0:SUBLANES]
    base = SUBLANES - taps
    first = tail_ref[base:base + SUBLANES, :] * conv_ref[0:1, :]
    for j in range(1, CONV_W):
        first = first + tail_ref[base + j:base + j + SUBLANES, :] * conv_ref[j:j + 1, :]
    qk_ref[0:SUBLANES, :] = first
    tail_ref[0:SUBLANES, 0:D_M] = q_ref[ln - 2 * SUBLANES:ln, :].astype(F32)[SUBLANES:2 * SUBLANES]
    tail_ref[0:SUBLANES, D_M:2 * D_M] = k_ref[ln - 2 * SUBLANES:ln, :].astype(F32)[SUBLANES:2 * SUBLANES]

    def conv_silu(col0):
        acc = qk_ref[:, col0:col0 + dh]
        return acc * _sigmoid(acc)

    gall = g_ref[...] + bg_ref[...]
    lane = lax.broadcasted_iota(jnp.int32, (ln, GATE_COLS), 1)
    is_f = (lane >= N_HEADS_M) & (lane < 2 * N_HEADS_M)
    lf = jnp.where(is_f, _log_sigmoid(gall), 0.0)
    row = lax.broadcasted_iota(jnp.int32, (ln, ln), 0)
    col = lax.broadcasted_iota(jnp.int32, (ln, ln), 1)
    causal = col <= row
    tri = jnp.where(causal, 1.0, 0.0).astype(BF16)
    lf_hi, lf_lo = _split_bf16(lf)
    b_all = _dot(tri, lf_hi) + _dot(tri, lf_lo)
    b_all_t = b_all.T
    gall_t = gall.T

    for h in range(N_HEADS_M):
        hs = slice(h * dh, (h + 1) * dh)
        qh = conv_silu(h * dh)
        kh = conv_silu(D_M + h * dh) * (HEAD_DIM_M ** -0.5)
        qb = qh.astype(BF16)
        kb = kh.astype(BF16)
        vb = v_ref[:, hs]

        b_col = b_all[:, N_HEADS_M + h:N_HEADS_M + h + 1]
        li_col = gall[:, h:h + 1]
        b_row = b_all_t[N_HEADS_M + h:N_HEADS_M + h + 1, :]
        li_row = gall_t[h:h + 1, :]
        m_prev = m_ref[h:h + 1, 0:1]

        d_log = jnp.where(causal, (b_col - b_row) + li_row, NEG_BIG)
        inter_log = b_col + m_prev
        m_t = jnp.maximum(jnp.max(d_log, axis=-1, keepdims=True), inter_log)
        scores = _dot_nt(qb, kb) * jnp.exp(d_log - m_t)
        inter_scale = jnp.exp(inter_log - m_t)
        c_prev = c_ref[h]
        n_prev = n_ref[h]
        num = _dot(scores.astype(BF16), vb) + inter_scale * _dot(qb, c_prev.astype(BF16))
        den = (jnp.sum(scores, axis=-1, keepdims=True)
               + inter_scale * jnp.sum(qh * n_prev, axis=-1, keepdims=True))
        hval = num / jnp.maximum(jnp.abs(den), jnp.exp(-m_t))

        g_last = b_col[ln - 1:ln, :]
        w_log = (g_last - b_col) + li_col
        m_new = jnp.maximum(g_last + m_prev, jnp.max(w_log, axis=0, keepdims=True))
        decay = jnp.exp((g_last + m_prev) - m_new)
        kw = kh * jnp.exp(w_log - m_new)
        c_ref[h] = decay * c_prev + _dot_tn(kw.astype(BF16), vb)
        n_ref[h] = decay * n_prev + jnp.sum(kw, axis=0, keepdims=True)
        m_ref[h:h + 1, :] = jnp.broadcast_to(m_new, (1, LANES))

        mu = jnp.mean(hval, axis=-1, keepdims=True)
        hc = hval - mu
        var = jnp.mean(hc * hc, axis=-1, keepdims=True)
        hn = hc * lax.rsqrt(var + LN_EPS) * norm_ref[:, hs]
        y_ref[:, hs] = (hn * _sigmoid(og_ref[:, hs].astype(F32))).astype(BF16)


def _mlstm(proj, gates, b_gates, conv_w, norm_g, bsz, seq):
    ln = MLSTM_CHUNK
    n_chunks = seq // ln
    col_blk = 2 * D_MODEL // D_M
    return pl.pallas_call(
        _mlstm_body,
        grid=(bsz, n_chunks),
        in_specs=[
            pl.BlockSpec((None, ln, D_M), lambda b, c: (b, c, col_blk)),
            pl.BlockSpec((None, ln, D_M), lambda b, c: (b, c, col_blk + 1)),
            pl.BlockSpec((None, ln, D_M), lambda b, c: (b, c, col_blk + 2)),
            pl.BlockSpec((None, ln, D_M), lambda b, c: (b, c, col_blk + 3)),
            pl.BlockSpec((None, ln, GATE_COLS), lambda b, c: (b, c, 0)),
            pl.BlockSpec((1, GATE_COLS), lambda b, c: (0, 0)),
            pl.BlockSpec((CONV_W, 2 * D_M), lambda b, c: (0, 0)),
            pl.BlockSpec((1, D_M), lambda b, c: (0, 0)),
        ],
        out_specs=pl.BlockSpec((None, ln, D_M), lambda b, c: (b, c, 0)),
        out_shape=jax.ShapeDtypeStruct((bsz, seq, D_M), BF16),
        scratch_shapes=[
            pltpu.VMEM((2 * SUBLANES, 2 * D_M), F32),
            pltpu.VMEM(((CONV_W - 1) * ln, ln), BF16),
            pltpu.VMEM((ln, 2 * D_M), F32),
            pltpu.VMEM((N_HEADS_M, HEAD_DIM_M, HEAD_DIM_M), F32),
            pltpu.VMEM((N_HEADS_M, 1, HEAD_DIM_M), F32),
            pltpu.VMEM((SUBLANES, LANES), F32),
        ],
        compiler_params=pltpu.CompilerParams(
            dimension_semantics=("parallel", "arbitrary"), vmem_limit_bytes=VMEM_LIMIT_BYTES),
        name="mlstm",
    )(proj, proj, proj, proj, gates, b_gates, conv_w, norm_g)


def _sb_body(q_ref, k_ref, v_ref, o_ref):
    tq = SB_TQ
    dh = HEAD_DIM_SB
    nh = SB_HEADS_PER_STEP
    qi = pl.program_id(2)
    z_scale = (HEAD_DIM_SB ** -0.5) * LOG2_E
    row = lax.broadcasted_iota(jnp.int32, (nh * tq, tq), 0) & (tq - 1)
    col = lax.broadcasted_iota(jnp.int32, (nh * tq, tq), 1)
    strict = col < row
    row2 = lax.broadcasted_iota(jnp.int32, (2 * tq, tq), 0)
    col2 = lax.broadcasted_iota(jnp.int32, (2 * tq, tq), 1)
    suffix2 = jnp.where((row2 & (tq - 1)) >= col2, 1.0, 0.0).astype(BF16)
    qs = [q_ref[:, hh * dh:(hh + 1) * dh] for hh in range(nh)]

    def block(kb, carry, accs, on_diagonal):
        off = pl.multiple_of(kb * tq, tq)
        z2 = jnp.concatenate(
            [_dot_nt(qs[hh], k_ref[pl.ds(off, tq), hh * dh:(hh + 1) * dh]) for hh in range(nh)], axis=0) * z_scale
        nz2 = -z2
        lom2 = jnp.minimum(nz2, 0.0) - jnp.log2(1.0 + jnp.exp2(jnp.minimum(z2, nz2)))
        if on_diagonal:
            lom2 = jnp.where(strict, lom2, 0.0)
        hi, lo = _split_bf16(lom2)
        rem = _dot(jnp.concatenate([hi, lo], axis=1), suffix2)
        att = jnp.exp2(z2 + (rem + carry))
        if on_diagonal:
            att = jnp.where(strict, att, 0.0)
        att = att.astype(BF16)
        accs = tuple(
            accs[hh] + _dot(att[hh * tq:(hh + 1) * tq], v_ref[pl.ds(off, tq), hh * dh:(hh + 1) * dh])
            for hh in range(nh))
        return carry + rem[:, 0:1], accs

    carry, accs = block(qi, jnp.zeros((nh * tq, 1), F32),
                        tuple(jnp.zeros((tq, dh), F32) for _ in range(nh)), True)

    def cond(loop):
        j, cmax = loop[0], loop[1]
        return jnp.logical_and(j < qi, cmax > SB_SKIP_LOG2)

    def step(loop):
        j = loop[0]
        new_carry, new_accs = block(qi - 1 - j, loop[2], loop[3:], False)
        return (j + 1, jnp.max(new_carry), new_carry) + new_accs

    final = lax.while_loop(cond, step, (jnp.int32(0), jnp.max(carry), carry) + accs)[3:]
    for hh in range(nh):
        o_ref[:, hh * dh:(hh + 1) * dh] = final[hh].astype(BF16)


def _stick_breaking(proj, bsz, seq):
    tq = SB_TQ
    width = SB_HEADS_PER_STEP * HEAD_DIM_SB
    q_blk = (2 * D_MODEL + 4 * D_M) // width
    k_blk = q_blk + D_SB // width
    v_blk = k_blk + D_SB // width
    return pl.pallas_call(
        _sb_body,
        grid=(bsz, D_SB // width, seq // tq),
        in_specs=[
            pl.BlockSpec((None, tq, width), lambda b, h, i: (b, i, q_blk + h)),
            pl.BlockSpec((None, seq, width), lambda b, h, i: (b, 0, k_blk + h)),
            pl.BlockSpec((None, seq, width), lambda b, h, i: (b, 0, v_blk + h)),
        ],
        out_specs=pl.BlockSpec((None, tq, width), lambda b, h, i: (b, i, h)),
        out_shape=jax.ShapeDtypeStruct((bsz, seq, D_SB), BF16),
        compiler_params=pltpu.CompilerParams(
            dimension_semantics=("parallel", "parallel", "arbitrary"), vmem_limit_bytes=VMEM_LIMIT_BYTES),
        name="stickbreak",
    )(proj, proj, proj)


def _merge_ln_body(ym_ref, ysb_ref, ga_ref, gb_ref, x_ref, wum_ref, wus_ref, wo_ref, g_ref, b_ref, o_ref):
    um = _dot(ym_ref[...], wum_ref[...])
    us = _dot(ysb_ref[...], wus_ref[...])
    merged = _sigmoid(ga_ref[...].astype(F32)) * um + _sigmoid(gb_ref[...].astype(F32)) * us
    mix = _dot(merged.astype(BF16), wo_ref[...])
    o_ref[...] = _layer_norm(ALPHA * x_ref[...] + mix, g_ref[...], b_ref[...])


def _merge_ln(ym, ysb, proj, x, w_up_m, w_up_sb, w_out, g, b):
    t = x.shape[0]
    tm = MERGE_TM
    const = dict(pipeline_mode=pl.Buffered(1))
    return pl.pallas_call(
        _merge_ln_body,
        grid=(t // tm,),
        in_specs=[
            pl.BlockSpec((tm, D_M), lambda i: (i, 0)),
            pl.BlockSpec((tm, D_SB), lambda i: (i, 0)),
            pl.BlockSpec((tm, D_MODEL), lambda i: (i, 0)),
            pl.BlockSpec((tm, D_MODEL), lambda i: (i, 1)),
            pl.BlockSpec((tm, D_MODEL), lambda i: (i, 0)),
            pl.BlockSpec((D_M, D_MODEL), lambda i: (0, 0), **const),
            pl.BlockSpec((D_SB, D_MODEL), lambda i: (0, 0), **const),
            pl.BlockSpec((D_MODEL, D_MODEL), lambda i: (0, 0), **const),
            pl.BlockSpec((1, D_MODEL), lambda i: (0, 0)),
            pl.BlockSpec((1, D_MODEL), lambda i: (0, 0)),
        ],
        out_specs=pl.BlockSpec((tm, D_MODEL), lambda i: (i, 0)),
        out_shape=jax.ShapeDtypeStruct((t, D_MODEL), F32),
        compiler_params=pltpu.CompilerParams(
            dimension_semantics=("parallel",), vmem_limit_bytes=VMEM_LIMIT_BYTES),
        name="merge_ln",
    )(ym, ysb, proj, proj, x, w_up_m, w_up_sb, w_out, g, b)


def _ple_body(x_ref, p_ref, wg_ref, wp_ref, o_ref):
    x = x_ref[...]
    gate = _sigmoid(_dot(x.astype(BF16), wg_ref[...]))
    o_ref[...] = x + gate * _dot(p_ref[...].astype(BF16), wp_ref[...])


def _ple(x, p, w_gate, w_proj):
    t = x.shape[0]
    tm = PLE_TM
    const = dict(pipeline_mode=pl.Buffered(1))
    return pl.pallas_call(
        _ple_body,
        grid=(t // tm,),
        in_specs=[
            pl.BlockSpec((tm, D_MODEL), lambda i: (i, 0)),
            pl.BlockSpec((tm, D_PLE), lambda i: (i, 0)),
            pl.BlockSpec((D_MODEL, D_MODEL), lambda i: (0, 0), **const),
            pl.BlockSpec((D_PLE, D_MODEL), lambda i: (0, 0), **const),
        ],
        out_specs=pl.BlockSpec((tm, D_MODEL), lambda i: (i, 0)),
        out_shape=jax.ShapeDtypeStruct((t, D_MODEL), F32),
        compiler_params=pltpu.CompilerParams(
            dimension_semantics=("parallel",), vmem_limit_bytes=VMEM_LIMIT_BYTES),
        name="ple",
    )(x, p, w_gate, w_proj)


def _split_w_in(w_in):
    sizes = [D_M, D_M, D_M, D_M, N_HEADS_M, N_HEADS_M, D_SB, D_SB, D_SB, D_MODEL, D_MODEL]
    parts, off = [], 0
    for s in sizes:
        parts.append(w_in[:, off:off + s])
        off += s
    return parts


def kernel(x, p, ffn1_w1, ffn1_w3, ffn1_w2, ln1_g, ln1_b, w_in, b_gates_m, conv_m, norm_m, w_up_m, w_up_sb, w_out, ln2_g, ln2_b, ffn2_w1, ffn2_w3, ffn2_w2, ln3_g, ln3_b, w_ple_gate, w_ple_proj):
    bsz, seq, _ = x.shape
    t = bsz * seq
    xf = x.reshape(t, D_MODEL)
    row = lambda v: v.reshape(1, -1)
    for i in range(DEPTH):
        xf, xb = _ffn_ln(xf, ffn1_w1[i].astype(BF16), ffn1_w3[i].astype(BF16), ffn1_w2[i].astype(BF16),
                         row(ln1_g[i]), row(ln1_b[i]), also_bf16=True)

        mq, mk, mv, mo, mi, mf, sq, sk, sv, ga, gb = _split_w_in(w_in[i])
        w_main = jnp.concatenate([ga, gb, mq, mk, mv, mo, sq, sk, sv], axis=1).astype(BF16)
        gate_pad = jnp.zeros((D_MODEL, GATE_COLS - 2 * N_HEADS_M), w_in.dtype)
        w_gates = jnp.concatenate([mi, mf, gate_pad], axis=1).astype(BF16)
        b_gates = jnp.concatenate([b_gates_m[i], jnp.zeros((GATE_COLS - 2 * N_HEADS_M,), F32)]).reshape(1, GATE_COLS)
        proj, gates = _in_proj(xb, w_main, w_gates)

        proj3 = proj.reshape(bsz, seq, PROJ_COLS)
        ym = _mlstm(proj3, gates.reshape(bsz, seq, GATE_COLS), b_gates, conv_m[i], row(norm_m[i]), bsz, seq)
        ysb = _stick_breaking(proj3, bsz, seq)

        xf = _merge_ln(ym.reshape(t, D_M), ysb.reshape(t, D_SB), proj, xf,
                       w_up_m[i].astype(BF16), w_up_sb[i].astype(BF16), w_out[i].astype(BF16),
                       row(ln2_g[i]), row(ln2_b[i]))
        (xf,) = _ffn_ln(xf, ffn2_w1[i].astype(BF16), ffn2_w3[i].astype(BF16), ffn2_w2[i].astype(BF16),
                        row(ln3_g[i]), row(ln3_b[i]), also_bf16=False)
        xf = _ple(xf, p[i].reshape(t, D_PLE), w_ple_gate[i].astype(BF16), w_ple_proj[i].astype(BF16))
    return xf.reshape(bsz, seq, D_MODEL)
```

```python
import functools

import jax
import jax.numpy as jnp
from jax import lax
from jax.experimental import pallas as pl
from jax.experimental.pallas import tpu as pltpu

D_MODEL = 2048
DEPTH = 1
N_HEADS_M = 4
HEAD_DIM_M = 256
D_M = N_HEADS_M * HEAD_DIM_M
N_HEADS_SB = 8
HEAD_DIM_SB = 128
D_SB = N_HEADS_SB * HEAD_DIM_SB
CONV_W = 4
D_FF = 5632
D_PLE = 256
ALPHA = (2.0 * DEPTH) ** 0.25
LN_EPS = 1e-5
NEG_BIG = -1e30

LANES = 128
SUBLANES = 8
VMEM_LIMIT_BYTES = 56 * 1024 * 1024

PROJ_COLS = 2 * D_MODEL + 4 * D_M + 3 * D_SB
GATE_COLS = LANES

BF16_SUBLANES = 16
FFN_TM = 512
FFN_TF = 512
FFN_LN_ROWS = 48
PROJ_TM = 2048
PROJ_TN = 1024
MLSTM_CHUNK = 256
SB_TQ = 256
SB_HEADS_PER_STEP = 4
LOG2_E = 1.4426950408889634
SB_SKIP_LOG2 = -160.0
MERGE_TM = 256
PLE_TM = 512

F32 = jnp.float32
BF16 = jnp.bfloat16


def _sigmoid(x):
    return 1.0 / (1.0 + jnp.exp(-x))


def _log_sigmoid(x):
    return jnp.minimum(x, 0.0) - jnp.log(1.0 + jnp.exp(-jnp.abs(x)))


def _layer_norm(y, g, b):
    mu = jnp.mean(y, axis=-1, keepdims=True)
    yc = y - mu
    var = jnp.mean(yc * yc, axis=-1, keepdims=True)
    return yc * lax.rsqrt(var + LN_EPS) * g + b


def _split_bf16(x):
    hi = x.astype(BF16)
    lo = (x - hi.astype(F32)).astype(BF16)
    return hi, lo


def _dot(a, b):
    return jnp.dot(a, b, preferred_element_type=F32)


def _dot_nt(a, b):
    return lax.dot_general(a, b, (((1,), (1,)), ((), ())), preferred_element_type=F32)


def _dot_tn(a, b):
    return lax.dot_general(a, b, (((0,), (0,)), ((), ())), preferred_element_type=F32)


def _ffn_ln_body(x_ref, w1_ref, w3_ref, w2_ref, g_ref, b_ref, o_ref, *rest):
    maybe_ob_ref, (xb_ref, acc_even_ref, acc_odd_ref) = rest[:-3], rest[-3:]
    i = pl.program_id(0)
    k = pl.program_id(1)
    n_tiles = pl.num_programs(0) - 1

    def ln_rows_of_previous_tile(acc_prev_ref):
        r0 = pl.multiple_of(jnp.minimum(k * FFN_LN_ROWS, FFN_TM - FFN_LN_ROWS), BF16_SUBLANES)
        rows = pl.ds(r0, FFN_LN_ROWS)
        out = _layer_norm(0.5 * acc_prev_ref[rows, :], g_ref[...], b_ref[...])
        o_ref[rows, :] = out
        for ob_ref in maybe_ob_ref:
            ob_ref[rows, :] = out.astype(BF16)

    def tile_step(acc_ref, acc_prev_ref):
        @pl.when(k == 0)
        def _():
            x = x_ref[...]
            xb_ref[...] = x.astype(BF16)
            acc_ref[...] = (2.0 * ALPHA) * x

        xb = xb_ref[...]
        h1 = _dot(xb, w1_ref[...])
        h3 = _dot(xb, w3_ref[...])
        h = (h1 * _sigmoid(h1)) * h3
        acc_ref[...] += _dot(h.astype(BF16), w2_ref[...])
        ln_rows_of_previous_tile(acc_prev_ref)

    @pl.when(jnp.logical_and(i == 0, k == 0))
    def _():
        acc_odd_ref[...] = jnp.zeros_like(acc_odd_ref)

    is_even = i % 2 == 0

    @pl.when(jnp.logical_and(i < n_tiles, is_even))
    def _():
        tile_step(acc_even_ref, acc_odd_ref)

    @pl.when(jnp.logical_and(i < n_tiles, jnp.logical_not(is_even)))
    def _():
        tile_step(acc_odd_ref, acc_even_ref)

    @pl.when(jnp.logical_and(i == n_tiles, is_even))
    def _():
        ln_rows_of_previous_tile(acc_odd_ref)

    @pl.when(jnp.logical_and(i == n_tiles, jnp.logical_not(is_even)))
    def _():
        ln_rows_of_previous_tile(acc_even_ref)


def _ffn_ln(x, w1, w3, w2, g, b, also_bf16):
    t = x.shape[0]
    n_tiles = t // FFN_TM
    n_k = D_FF // FFN_TF
    assert n_k * FFN_LN_ROWS >= FFN_TM and FFN_LN_ROWS % BF16_SUBLANES == 0
    n_out = 2 if also_bf16 else 1
    out_dtypes = (F32, BF16)[:n_out]
    k_of = lambda i, k: jnp.where(i == n_tiles, n_k - 1, k)
    return pl.pallas_call(
        _ffn_ln_body,
        grid=(n_tiles + 1, n_k),
        in_specs=[
            pl.BlockSpec((FFN_TM, D_MODEL), lambda i, k: (jnp.minimum(i, n_tiles - 1), 0)),
            pl.BlockSpec((D_MODEL, FFN_TF), lambda i, k: (0, k_of(i, k))),
            pl.BlockSpec((D_MODEL, FFN_TF), lambda i, k: (0, k_of(i, k))),
            pl.BlockSpec((FFN_TF, D_MODEL), lambda i, k: (k_of(i, k), 0)),
            pl.BlockSpec((1, D_MODEL), lambda i, k: (0, 0)),
            pl.BlockSpec((1, D_MODEL), lambda i, k: (0, 0)),
        ],
        out_specs=[pl.BlockSpec((FFN_TM, D_MODEL), lambda i, k: (jnp.maximum(i - 1, 0), 0)) for _ in out_dtypes],
        out_shape=[jax.ShapeDtypeStruct((t, D_MODEL), dt) for dt in out_dtypes],
        scratch_shapes=[pltpu.VMEM((FFN_TM, D_MODEL), BF16),
                        pltpu.VMEM((FFN_TM, D_MODEL), F32), pltpu.VMEM((FFN_TM, D_MODEL), F32)],
        compiler_params=pltpu.CompilerParams(
            dimension_semantics=("arbitrary", "arbitrary"), vmem_limit_bytes=VMEM_LIMIT_BYTES),
        name="ffn_ln",
    )(x, w1, w3, w2, g, b)


def _in_proj_body(x_ref, w_ref, wg_ref, o_ref, g_ref):
    @pl.when(pl.program_id(1) == 0)
    def _():
        g_ref[...] = _dot(x_ref[...], wg_ref[...])

    o_ref[...] = _dot(x_ref[...], w_ref[...]).astype(BF16)


def _in_proj(x, w_main, w_gates):
    t = x.shape[0]
    grid = (t // PROJ_TM, PROJ_COLS // PROJ_TN)
    return pl.pallas_call(
        _in_proj_body,
        grid=grid,
        in_specs=[
            pl.BlockSpec((PROJ_TM, D_MODEL), lambda i, j: (i, 0)),
            pl.BlockSpec((D_MODEL, PROJ_TN), lambda i, j: (0, j)),
            pl.BlockSpec((D_MODEL, GATE_COLS), lambda i, j: (0, 0)),
        ],
        out_specs=[
            pl.BlockSpec((PROJ_TM, PROJ_TN), lambda i, j: (i, j)),
            pl.BlockSpec((PROJ_TM, GATE_COLS), lambda i, j: (i, 0)),
        ],
        out_shape=[
            jax.ShapeDtypeStruct((t, PROJ_COLS), BF16),
            jax.ShapeDtypeStruct((t, GATE_COLS), F32),
        ],
        compiler_params=pltpu.CompilerParams(
            dimension_semantics=("parallel", "arbitrary"), vmem_limit_bytes=VMEM_LIMIT_BYTES),
        name="in_proj",
    )(x, w_main, w_gates)


def _mlstm_body(q_ref, k_ref, v_ref, og_ref, g_ref, bg_ref, conv_ref, norm_ref, y_ref,
                tail_ref, shift_ref, qk_ref, c_ref, n_ref, m_ref):
    ln = MLSTM_CHUNK
    dh = HEAD_DIM_M
    taps = CONV_W - 1

    @pl.when(pl.program_id(1) == 0)
    def _():
        tail_ref[0:SUBLANES, :] = jnp.zeros((SUBLANES, 2 * D_M), F32)
        c_ref[...] = jnp.zeros_like(c_ref)
        n_ref[...] = jnp.zeros_like(n_ref)
        m_ref[...] = jnp.full_like(m_ref, NEG_BIG)
        r = lax.broadcasted_iota(jnp.int32, (taps * ln, ln), 0)
        s = lax.broadcasted_iota(jnp.int32, (taps * ln, ln), 1)
        shift_ref[...] = jnp.where(s == (r & (ln - 1)) - (r // ln + 1), 1.0, 0.0).astype(BF16)

    for part, x_ref in enumerate((q_ref, k_ref)):
        for h in range(N_HEADS_M):
            cs = slice(part * D_M + h * dh, part * D_M + (h + 1) * dh)
            x = x_ref[:, h * dh:(h + 1) * dh]
            sh = _dot(shift_ref[...], x)
            acc = sh[(taps - 1) * ln:taps * ln] * conv_ref[0:1, cs]
            for j in range(1, taps):
                acc = acc + sh[(taps - 1 - j) * ln:(taps - j) * ln] * conv_ref[j:j + 1, cs]
            qk_ref[:, cs] = acc + x.astype(F32) * conv_ref[taps:taps + 1, cs]
    tail_ref[SUBLANES:2 * SUBLANES, 0:D_M] = q_ref[0:2 * SUBLANES, :].astype(F32)[0:SUBLANES]
    tail_ref[SUBLANES:2 * SUBLANES, D_M:2 * D_M] = k_ref[0:2 * SUBLANES, :].astype(F32)[0:SUBLANES]
    base = SUBLANES - taps
    first = tail_ref[base:base + SUBLANES, :] * conv_ref[0:1, :]
    for j in range(1, CONV_W):
        first = first + tail_ref[base + j:base + j + SUBLANES, :] * conv_ref[j:j + 1, :]
    qk_ref[0:SUBLANES, :] = first
    tail_ref[0:SUBLANES, 0:D_M] = q_ref[ln - 2 * SUBLANES:ln, :].astype(F32)[SUBLANES:2 * SUBLANES]
    tail_ref[0:SUBLANES, D_M:2 * D_M] = k_ref[ln - 2 * SUBLANES:ln, :].astype(F32)[SUBLANES:2 * SUBLANES]

    def conv_silu(col0):
        acc = qk_ref[:, col0:col0 + dh]
        return acc * _sigmoid(acc)

    gall = g_ref[...] + bg_ref[...]
    lane = lax.broadcasted_iota(jnp.int32, (ln, GATE_COLS), 1)
    is_f = (lane >= N_HEADS_M) & (lane < 2 * N_HEADS_M)
    lf = jnp.where(is_f, _log_sigmoid(gall), 0.0)
    row = lax.broadcasted_iota(jnp.int32, (ln, ln), 0)
    col = lax.broadcasted_iota(jnp.int32, (ln, ln), 1)
    causal = col <= row
    tri = jnp.where(causal, 1.0, 0.0).astype(BF16)
    lf_hi, lf_lo = _split_bf16(lf)
    b_all = _dot(tri, lf_hi) + _dot(tri, lf_lo)
    b_all_t = b_all.T
    gall_t = gall.T

    for h in range(N_HEADS_M):
        hs = slice(h * dh, (h + 1) * dh)
        qh = conv_silu(h * dh)
        kh = conv_silu(D_M + h * dh) * (HEAD_DIM_M ** -0.5)
        qb = qh.astype(BF16)
        kb = kh.astype(BF16)
        vb = v_ref[:, hs]

        b_col = b_all[:, N_HEADS_M + h:N_HEADS_M + h + 1]
        li_col = gall[:, h:h + 1]
        b_row = b_all_t[N_HEADS_M + h:N_HEADS_M + h + 1, :]
        li_row = gall_t[h:h + 1, :]
        m_prev = m_ref[h:h + 1, 0:1]

        d_log = jnp.where(causal, (b_col - b_row) + li_row, NEG_BIG)
        inter_log = b_col + m_prev
        m_t = jnp.maximum(jnp.max(d_log, axis=-1, keepdims=True), inter_log)
        scores = _dot_nt(qb, kb) * jnp.exp(d_log - m_t)
        inter_scale = jnp.exp(inter_log - m_t)
        c_prev = c_ref[h]
        n_prev = n_ref[h]
        num = _dot(scores.astype(BF16), vb) + inter_scale * _dot(qb, c_prev.astype(BF16))
        den = (jnp.sum(scores, axis=-1, keepdims=True)
               + inter_scale * jnp.sum(qh * n_prev, axis=-1, keepdims=True))
        hval = num / jnp.maximum(jnp.abs(den), jnp.exp(-m_t))

        g_last = b_col[ln - 1:ln, :]
        w_log = (g_last - b_col) + li_col
        m_new = jnp.maximum(g_last + m_prev, jnp.max(w_log, axis=0, keepdims=True))
        decay = jnp.exp((g_last + m_prev) - m_new)
        kw = kh * jnp.exp(w_log - m_new)
        c_ref[h] = decay * c_prev + _dot_tn(kw.astype(BF16), vb)
        n_ref[h] = decay * n_prev + jnp.sum(kw, axis=0, keepdims=True)
        m_ref[h:h + 1, :] = jnp.broadcast_to(m_new, (1, LANES))

        mu = jnp.mean(hval, axis=-1, keepdims=True)
        hc = hval - mu
        var = jnp.mean(hc * hc, axis=-1, keepdims=True)
        hn = hc * lax.rsqrt(var + LN_EPS) * norm_ref[:, hs]
        y_ref[:, hs] = (hn * _sigmoid(og_ref[:, hs].astype(F32))).astype(BF16)


def _mlstm(proj, gates, b_gates, conv_w, norm_g, bsz, seq):
    ln = MLSTM_CHUNK
    n_chunks = seq // ln
    col_blk = 2 * D_MODEL // D_M
    return pl.pallas_call(
        _mlstm_body,
        grid=(bsz, n_chunks),
        in_specs=[
            pl.BlockSpec((None, ln, D_M), lambda b, c: (b, c, col_blk)),
            pl.BlockSpec((None, ln, D_M), lambda b, c: (b, c, col_blk + 1)),
            pl.BlockSpec((None, ln, D_M), lambda b, c: (b, c, col_blk + 2)),
            pl.BlockSpec((None, ln, D_M), lambda b, c: (b, c, col_blk + 3)),
            pl.BlockSpec((None, ln, GATE_COLS), lambda b, c: (b, c, 0)),
            pl.BlockSpec((1, GATE_COLS), lambda b, c: (0, 0)),
            pl.BlockSpec((CONV_W, 2 * D_M), lambda b, c: (0, 0)),
            pl.BlockSpec((1, D_M), lambda b, c: (0, 0)),
        ],
        out_specs=pl.BlockSpec((None, ln, D_M), lambda b, c: (b, c, 0)),
        out_shape=jax.ShapeDtypeStruct((bsz, seq, D_M), BF16),
        scratch_shapes=[
            pltpu.VMEM((2 * SUBLANES, 2 * D_M), F32),
            pltpu.VMEM(((CONV_W - 1) * ln, ln), BF16),
            pltpu.VMEM((ln, 2 * D_M), F32),
            pltpu.VMEM((N_HEADS_M, HEAD_DIM_M, HEAD_DIM_M), F32),
            pltpu.VMEM((N_HEADS_M, 1, HEAD_DIM_M), F32),
            pltpu.VMEM((SUBLANES, LANES), F32),
        ],
        compiler_params=pltpu.CompilerParams(
            dimension_semantics=("parallel", "arbitrary"), vmem_limit_bytes=VMEM_LIMIT_BYTES),
        name="mlstm",
    )(proj, proj, proj, proj, gates, b_gates, conv_w, norm_g)


def _sb_body(q_ref, k_ref, v_ref, o_ref):
    tq = SB_TQ
    dh = HEAD_DIM_SB
    nh = SB_HEADS_PER_STEP
    qi = pl.program_id(2)
    z_scale = (HEAD_DIM_SB ** -0.5) * LOG2_E
    row = lax.broadcasted_iota(jnp.int32, (nh * tq, tq), 0) & (tq - 1)
    col = lax.broadcasted_iota(jnp.int32, (nh * tq, tq), 1)
    strict = col < row
    row2 = lax.broadcasted_iota(jnp.int32, (2 * tq, tq), 0)
    col2 = lax.broadcasted_iota(jnp.int32, (2 * tq, tq), 1)
    suffix2 = jnp.where((row2 & (tq - 1)) >= col2, 1.0, 0.0).astype(BF16)
    qs = [q_ref[:, hh * dh:(hh + 1) * dh] for hh in range(nh)]

    def block(kb, carry, accs, on_diagonal):
        off = pl.multiple_of(kb * tq, tq)
        z2 = jnp.concatenate(
            [_dot_nt(qs[hh], k_ref[pl.ds(off, tq), hh * dh:(hh + 1) * dh]) for hh in range(nh)], axis=0) * z_scale
        nz2 = -z2
        lom2 = jnp.minimum(nz2, 0.0) - jnp.log2(1.0 + jnp.exp2(jnp.minimum(z2, nz2)))
        if on_diagonal:
            lom2 = jnp.where(strict, lom2, 0.0)
        hi, lo = _split_bf16(lom2)
        rem = _dot(jnp.concatenate([hi, lo], axis=1), suffix2)
        att = jnp.exp2(z2 + (rem + carry))
        if on_diagonal:
            att = jnp.where(strict, att, 0.0)
        att = att.astype(BF16)
        accs = tuple(
            accs[hh] + _dot(att[hh * tq:(hh + 1) * tq], v_ref[pl.ds(off, tq), hh * dh:(hh + 1) * dh])
            for hh in range(nh))
        return carry + rem[:, 0:1], accs

    carry, accs = block(qi, jnp.zeros((nh * tq, 1), F32),
                        tuple(jnp.zeros((tq, dh), F32) for _ in range(nh)), True)

    def cond(loop):
        j, cmax = loop[0], loop[1]
        return jnp.logical_and(j < qi, cmax > SB_SKIP_LOG2)

    def step(loop):
        j = loop[0]
        new_carry, new_accs = block(qi - 1 - j, loop[2], loop[3:], False)
        return (j + 1, jnp.max(new_carry), new_carry) + new_accs

    final = lax.while_loop(cond, step, (jnp.int32(0), jnp.max(carry), carry) + accs)[3:]
    for hh in range(nh):
        o_ref[:, hh * dh:(hh + 1) * dh] = final[hh].astype(BF16)


def _stick_breaking(proj, bsz, seq):
    tq = SB_TQ
    width = SB_HEADS_PER_STEP * HEAD_DIM_SB
    q_blk = (2 * D_MODEL + 4 * D_M) // width
    k_blk = q_blk + D_SB // width
    v_blk = k_blk + D_SB // width
    return pl.pallas_call(
        _sb_body,
        grid=(bsz, D_SB // width, seq // tq),
        in_specs=[
            pl.BlockSpec((None, tq, width), lambda b, h, i: (b, i, q_blk + h)),
            pl.BlockSpec((None, seq, width), lambda b, h, i: (b, 0, k_blk + h)),
            pl.BlockSpec((None, seq, width), lambda b, h, i: (b, 0, v_blk + h)),
        ],
        out_specs=pl.BlockSpec((None, tq, width), lambda b, h, i: (b, i, h)),
        out_shape=jax.ShapeDtypeStruct((bsz, seq, D_SB), BF16),
        compiler_params=pltpu.CompilerParams(
            dimension_semantics=("parallel", "parallel", "arbitrary"), vmem_limit_bytes=VMEM_LIMIT_BYTES),
        name="stickbreak",
    )(proj, proj, proj)


def _merge_ln_body(ym_ref, ysb_ref, ga_ref, gb_ref, x_ref, wum_ref, wus_ref, wo_ref, g_ref, b_ref, o_ref,
                   y_even_ref, y_odd_ref):
    i = pl.program_id(0)
    n_tiles = pl.num_programs(0) - 1

    def ln_of_previous_tile(y_prev_ref):
        o_ref[...] = _layer_norm(y_prev_ref[...], g_ref[...], b_ref[...])

    def tile_step(y_ref, y_prev_ref):
        um = _dot(ym_ref[...], wum_ref[...])
        us = _dot(ysb_ref[...], wus_ref[...])
        merged = _sigmoid(ga_ref[...].astype(F32)) * um + _sigmoid(gb_ref[...].astype(F32)) * us
        y_ref[...] = ALPHA * x_ref[...] + _dot(merged.astype(BF16), wo_ref[...])
        ln_of_previous_tile(y_prev_ref)

    @pl.when(i == 0)
    def _():
        y_odd_ref[...] = jnp.zeros_like(y_odd_ref)

    is_even = i % 2 == 0

    @pl.when(jnp.logical_and(i < n_tiles, is_even))
    def _():
        tile_step(y_even_ref, y_odd_ref)

    @pl.when(jnp.logical_and(i < n_tiles, jnp.logical_not(is_even)))
    def _():
        tile_step(y_odd_ref, y_even_ref)

    @pl.when(jnp.logical_and(i == n_tiles, is_even))
    def _():
        ln_of_previous_tile(y_odd_ref)

    @pl.when(jnp.logical_and(i == n_tiles, jnp.logical_not(is_even)))
    def _():
        ln_of_previous_tile(y_even_ref)


def _merge_ln(ym, ysb, proj, x, w_up_m, w_up_sb, w_out, g, b):
    t = x.shape[0]
    tm = MERGE_TM
    n_tiles = t // tm
    const = dict(pipeline_mode=pl.Buffered(1))
    cur = lambda i: jnp.minimum(i, n_tiles - 1)
    return pl.pallas_call(
        _merge_ln_body,
        grid=(n_tiles + 1,),
        in_specs=[
            pl.BlockSpec((tm, D_M), lambda i: (cur(i), 0)),
            pl.BlockSpec((tm, D_SB), lambda i: (cur(i), 0)),
            pl.BlockSpec((tm, D_MODEL), lambda i: (cur(i), 0)),
            pl.BlockSpec((tm, D_MODEL), lambda i: (cur(i), 1)),
            pl.BlockSpec((tm, D_MODEL), lambda i: (cur(i), 0)),
            pl.BlockSpec((D_M, D_MODEL), lambda i: (0, 0), **const),
            pl.BlockSpec((D_SB, D_MODEL), lambda i: (0, 0), **const),
            pl.BlockSpec((D_MODEL, D_MODEL), lambda i: (0, 0), **const),
            pl.BlockSpec((1, D_MODEL), lambda i: (0, 0)),
            pl.BlockSpec((1, D_MODEL), lambda i: (0, 0)),
        ],
        out_specs=pl.BlockSpec((tm, D_MODEL), lambda i: (jnp.maximum(i - 1, 0), 0)),
        out_shape=jax.ShapeDtypeStruct((t, D_MODEL), F32),
        scratch_shapes=[pltpu.VMEM((tm, D_MODEL), F32), pltpu.VMEM((tm, D_MODEL), F32)],
        compiler_params=pltpu.CompilerParams(
            dimension_semantics=("arbitrary",), vmem_limit_bytes=VMEM_LIMIT_BYTES),
        name="merge_ln",
    )(ym, ysb, proj, proj, x, w_up_m, w_up_sb, w_out, g, b)


def _ple_body(x_ref, p_ref, wg_ref, wp_ref, o_ref):
    x = x_ref[...]
    gate = _sigmoid(_dot(x.astype(BF16), wg_ref[...]))
    o_ref[...] = x + gate * _dot(p_ref[...].astype(BF16), wp_ref[...])


def _ple(x, p, w_gate, w_proj):
    t = x.shape[0]
    tm = PLE_TM
    const = dict(pipeline_mode=pl.Buffered(1))
    return pl.pallas_call(
        _ple_body,
        grid=(t // tm,),
        in_specs=[
            pl.BlockSpec((tm, D_MODEL), lambda i: (i, 0)),
            pl.BlockSpec((tm, D_PLE), lambda i: (i, 0)),
            pl.BlockSpec((D_MODEL, D_MODEL), lambda i: (0, 0), **const),
            pl.BlockSpec((D_PLE, D_MODEL), lambda i: (0, 0), **const),
        ],
        out_specs=pl.BlockSpec((tm, D_MODEL), lambda i: (i, 0)),
        out_shape=jax.ShapeDtypeStruct((t, D_MODEL), F32),
        compiler_params=pltpu.CompilerParams(
            dimension_semantics=("parallel",), vmem_limit_bytes=VMEM_LIMIT_BYTES),
        name="ple",
    )(x, p, w_gate, w_proj)


def _split_w_in(w_in):
    sizes = [D_M, D_M, D_M, D_M, N_HEADS_M, N_HEADS_M, D_SB, D_SB, D_SB, D_MODEL, D_MODEL]
    parts, off = [], 0
    for s in sizes:
        parts.append(w_in[:, off:off + s])
        off += s
    return parts


def kernel(x, p, ffn1_w1, ffn1_w3, ffn1_w2, ln1_g, ln1_b, w_in, b_gates_m, conv_m, norm_m, w_up_m, w_up_sb, w_out, ln2_g, ln2_b, ffn2_w1, ffn2_w3, ffn2_w2, ln3_g, ln3_b, w_ple_gate, w_ple_proj):
    bsz, seq, _ = x.shape
    t = bsz * seq
    xf = x.reshape(t, D_MODEL)
    row = lambda v: v.reshape(1, -1)
    for i in range(DEPTH):
        xf, xb = _ffn_ln(xf, ffn1_w1[i].astype(BF16), ffn1_w3[i].astype(BF16), ffn1_w2[i].astype(BF16),
                         row(ln1_g[i]), row(ln1_b[i]), also_bf16=True)

        mq, mk, mv, mo, mi, mf, sq, sk, sv, ga, gb = _split_w_in(w_in[i])
        w_main = jnp.concatenate([ga, gb, mq, mk, mv, mo, sq, sk, sv], axis=1).astype(BF16)
        gate_pad = jnp.zeros((D_MODEL, GATE_COLS - 2 * N_HEADS_M), w_in.dtype)
        w_gates = jnp.concatenate([mi, mf, gate_pad], axis=1).astype(BF16)
        b_gates = jnp.concatenate([b_gates_m[i], jnp.zeros((GATE_COLS - 2 * N_HEADS_M,), F32)]).reshape(1, GATE_COLS)
        proj, gates = _in_proj(xb, w_main, w_gates)

        proj3 = proj.reshape(bsz, seq, PROJ_COLS)
        ym = _mlstm(proj3, gates.reshape(bsz, seq, GATE_COLS), b_gates, conv_m[i], row(norm_m[i]), bsz, seq)
        ysb = _stick_breaking(proj3, bsz, seq)

        xf = _merge_ln(ym.reshape(t, D_M), ysb.reshape(t, D_SB), proj, xf,
                       w_up_m[i].astype(BF16), w_up_sb[i].astype(BF16), w_out[i].astype(BF16),
                       row(ln2_g[i]), row(ln2_b[i]))
        (xf,) = _ffn_ln(xf, ffn2_w1[i].astype(BF16), ffn2_w3[i].astype(BF16), ffn2_w2[i].astype(BF16),
                        row(ln3_g[i]), row(ln3_b[i]), also_bf16=False)
        xf = _ple(xf, p[i].reshape(t, D_PLE), w_ple_gate[i].astype(BF16), w_ple_proj[i].astype(BF16))
    return xf.reshape(bsz, seq, D_MODEL)
```

```python
import functools

import jax
import jax.numpy as jnp
from jax import lax
from jax.experimental import pallas as pl
from jax.experimental.pallas import tpu as pltpu

D_MODEL = 2048
DEPTH = 1
N_HEADS_M = 4
HEAD_DIM_M = 256
D_M = N_HEADS_M * HEAD_DIM_M
N_HEADS_SB = 8
HEAD_DIM_SB = 128
D_SB = N_HEADS_SB * HEAD_DIM_SB
CONV_W = 4
D_FF = 5632
D_PLE = 256
ALPHA = (2.0 * DEPTH) ** 0.25
LN_EPS = 1e-5
NEG_BIG = -1e30

LANES = 128
SUBLANES = 8
VMEM_LIMIT_BYTES = 56 * 1024 * 1024

PROJ_COLS = 4 * D_M + 3 * D_SB + 2 * D_MODEL
PROJ_SB_COL = 4 * D_M
PROJ_GATE_COL = 4 * D_M + 3 * D_SB
GATE_COLS = LANES

BF16_SUBLANES = 16
FFN_TM = 512
FFN_TF = 512
FFN_LN_ROWS = 48
PROJ_TM = 2048
PROJ_TN = 1024
MLSTM_CHUNK = 256
SB_TQ = 256
SB_HEADS_PER_STEP = 4
LOG2_E = 1.4426950408889634
SB_SKIP_LOG2 = -160.0
MERGE_TM = 256
PLE_TM = 512

F32 = jnp.float32
BF16 = jnp.bfloat16


def _sigmoid(x):
    return 1.0 / (1.0 + jnp.exp(-x))


def _log_sigmoid(x):
    return jnp.minimum(x, 0.0) - jnp.log(1.0 + jnp.exp(-jnp.abs(x)))


def _layer_norm(y, g, b):
    mu = jnp.mean(y, axis=-1, keepdims=True)
    yc = y - mu
    var = jnp.mean(yc * yc, axis=-1, keepdims=True)
    return yc * lax.rsqrt(var + LN_EPS) * g + b


def _split_bf16(x):
    hi = x.astype(BF16)
    lo = (x - hi.astype(F32)).astype(BF16)
    return hi, lo


def _dot(a, b):
    return jnp.dot(a, b, preferred_element_type=F32)


def _dot_nt(a, b):
    return lax.dot_general(a, b, (((1,), (1,)), ((), ())), preferred_element_type=F32)


def _dot_tn(a, b):
    return lax.dot_general(a, b, (((0,), (0,)), ((), ())), preferred_element_type=F32)


def _ffn_ln_body(x_ref, w1_ref, w3_ref, w2_ref, g_ref, b_ref, o_ref, *rest):
    maybe_ob_ref, (xb_ref, acc_even_ref, acc_odd_ref) = rest[:-3], rest[-3:]
    i = pl.program_id(0)
    k = pl.program_id(1)
    n_tiles = pl.num_programs(0) - 1

    def ln_rows_of_previous_tile(acc_prev_ref):
        r0 = pl.multiple_of(jnp.minimum(k * FFN_LN_ROWS, FFN_TM - FFN_LN_ROWS), BF16_SUBLANES)
        rows = pl.ds(r0, FFN_LN_ROWS)
        out = _layer_norm(0.5 * acc_prev_ref[rows, :], g_ref[...], b_ref[...])
        o_ref[rows, :] = out
        for ob_ref in maybe_ob_ref:
            ob_ref[rows, :] = out.astype(BF16)

    def tile_step(acc_ref, acc_prev_ref):
        @pl.when(k == 0)
        def _():
            x = x_ref[...]
            xb_ref[...] = x.astype(BF16)
            acc_ref[...] = (2.0 * ALPHA) * x

        xb = xb_ref[...]
        h1 = _dot(xb, w1_ref[...])
        h3 = _dot(xb, w3_ref[...])
        h = (h1 * _sigmoid(h1)) * h3
        acc_ref[...] += _dot(h.astype(BF16), w2_ref[...])
        ln_rows_of_previous_tile(acc_prev_ref)

    @pl.when(jnp.logical_and(i == 0, k == 0))
    def _():
        acc_odd_ref[...] = jnp.zeros_like(acc_odd_ref)

    is_even = i % 2 == 0

    @pl.when(jnp.logical_and(i < n_tiles, is_even))
    def _():
        tile_step(acc_even_ref, acc_odd_ref)

    @pl.when(jnp.logical_and(i < n_tiles, jnp.logical_not(is_even)))
    def _():
        tile_step(acc_odd_ref, acc_even_ref)

    @pl.when(jnp.logical_and(i == n_tiles, is_even))
    def _():
        ln_rows_of_previous_tile(acc_odd_ref)

    @pl.when(jnp.logical_and(i == n_tiles, jnp.logical_not(is_even)))
    def _():
        ln_rows_of_previous_tile(acc_even_ref)


def _ffn_ln(x, w1, w3, w2, g, b, also_bf16):
    t = x.shape[0]
    n_tiles = t // FFN_TM
    n_k = D_FF // FFN_TF
    assert n_k * FFN_LN_ROWS >= FFN_TM and FFN_LN_ROWS % BF16_SUBLANES == 0
    n_out = 2 if also_bf16 else 1
    out_dtypes = (F32, BF16)[:n_out]
    k_of = lambda i, k: jnp.where(i == n_tiles, n_k - 1, k)
    return pl.pallas_call(
        _ffn_ln_body,
        grid=(n_tiles + 1, n_k),
        in_specs=[
            pl.BlockSpec((FFN_TM, D_MODEL), lambda i, k: (jnp.minimum(i, n_tiles - 1), 0)),
            pl.BlockSpec((D_MODEL, FFN_TF), lambda i, k: (0, k_of(i, k))),
            pl.BlockSpec((D_MODEL, FFN_TF), lambda i, k: (0, k_of(i, k))),
            pl.BlockSpec((FFN_TF, D_MODEL), lambda i, k: (k_of(i, k), 0)),
            pl.BlockSpec((1, D_MODEL), lambda i, k: (0, 0)),
            pl.BlockSpec((1, D_MODEL), lambda i, k: (0, 0)),
        ],
        out_specs=[pl.BlockSpec((FFN_TM, D_MODEL), lambda i, k: (jnp.maximum(i - 1, 0), 0)) for _ in out_dtypes],
        out_shape=[jax.ShapeDtypeStruct((t, D_MODEL), dt) for dt in out_dtypes],
        scratch_shapes=[pltpu.VMEM((FFN_TM, D_MODEL), BF16),
                        pltpu.VMEM((FFN_TM, D_MODEL), F32), pltpu.VMEM((FFN_TM, D_MODEL), F32)],
        compiler_params=pltpu.CompilerParams(
            dimension_semantics=("arbitrary", "arbitrary"), vmem_limit_bytes=VMEM_LIMIT_BYTES),
        name="ffn_ln",
    )(x, w1, w3, w2, g, b)


def _in_proj_body(x_ref, w_ref, wg_ref, o_ref, g_ref):
    @pl.when(pl.program_id(1) == 0)
    def _():
        g_ref[...] = _dot(x_ref[...], wg_ref[...])

    o_ref[...] = _dot(x_ref[...], w_ref[...]).astype(BF16)


def _in_proj(x, w_main, w_gates):
    t = x.shape[0]
    grid = (t // PROJ_TM, PROJ_COLS // PROJ_TN)
    return pl.pallas_call(
        _in_proj_body,
        grid=grid,
        in_specs=[
            pl.BlockSpec((PROJ_TM, D_MODEL), lambda i, j: (i, 0)),
            pl.BlockSpec((D_MODEL, PROJ_TN), lambda i, j: (0, j)),
            pl.BlockSpec((D_MODEL, GATE_COLS), lambda i, j: (0, 0)),
        ],
        out_specs=[
            pl.BlockSpec((PROJ_TM, PROJ_TN), lambda i, j: (i, j)),
            pl.BlockSpec((PROJ_TM, GATE_COLS), lambda i, j: (i, 0)),
        ],
        out_shape=[
            jax.ShapeDtypeStruct((t, PROJ_COLS), BF16),
            jax.ShapeDtypeStruct((t, GATE_COLS), F32),
        ],
        compiler_params=pltpu.CompilerParams(
            dimension_semantics=("parallel", "arbitrary"), vmem_limit_bytes=VMEM_LIMIT_BYTES),
        name="in_proj",
    )(x, w_main, w_gates)


def _mlstm_body(q_ref, k_ref, v_ref, og_ref, g_ref, bg_ref, conv_ref, norm_ref, y_ref,
                tail_ref, shift_ref, qk_ref, c_ref, n_ref, m_ref):
    ln = MLSTM_CHUNK
    dh = HEAD_DIM_M
    taps = CONV_W - 1

    @pl.when(pl.program_id(1) == 0)
    def _():
        tail_ref[0:SUBLANES, :] = jnp.zeros((SUBLANES, 2 * D_M), F32)
        c_ref[...] = jnp.zeros_like(c_ref)
        n_ref[...] = jnp.zeros_like(n_ref)
        m_ref[...] = jnp.full_like(m_ref, NEG_BIG)
        r = lax.broadcasted_iota(jnp.int32, (taps * ln, ln), 0)
        s = lax.broadcasted_iota(jnp.int32, (taps * ln, ln), 1)
        shift_ref[...] = jnp.where(s == (r & (ln - 1)) - (r // ln + 1), 1.0, 0.0).astype(BF16)

    for part, x_ref in enumerate((q_ref, k_ref)):
        for h in range(N_HEADS_M):
            cs = slice(part * D_M + h * dh, part * D_M + (h + 1) * dh)
            x = x_ref[:, h * dh:(h + 1) * dh]
            sh = _dot(shift_ref[...], x)
            acc = sh[(taps - 1) * ln:taps * ln] * conv_ref[0:1, cs]
            for j in range(1, taps):
                acc = acc + sh[(taps - 1 - j) * ln:(taps - j) * ln] * conv_ref[j:j + 1, cs]
            qk_ref[:, cs] = acc + x.astype(F32) * conv_ref[taps:taps + 1, cs]
    tail_ref[SUBLANES:2 * SUBLANES, 0:D_M] = q_ref[0:2 * SUBLANES, :].astype(F32)[0:SUBLANES]
    tail_ref[SUBLANES:2 * SUBLANES, D_M:2 * D_M] = k_ref[0:2 * SUBLANES, :].astype(F32)[0:SUBLANES]
    base = SUBLANES - taps
    first = tail_ref[base:base + SUBLANES, :] * conv_ref[0:1, :]
    for j in range(1, CONV_W):
        first = first + tail_ref[base + j:base + j + SUBLANES, :] * conv_ref[j:j + 1, :]
    qk_ref[0:SUBLANES, :] = first
    tail_ref[0:SUBLANES, 0:D_M] = q_ref[ln - 2 * SUBLANES:ln, :].astype(F32)[SUBLANES:2 * SUBLANES]
    tail_ref[0:SUBLANES, D_M:2 * D_M] = k_ref[ln - 2 * SUBLANES:ln, :].astype(F32)[SUBLANES:2 * SUBLANES]

    def conv_silu(col0):
        acc = qk_ref[:, col0:col0 + dh]
        return acc * _sigmoid(acc)

    gall = g_ref[...] + bg_ref[...]
    lane = lax.broadcasted_iota(jnp.int32, (ln, GATE_COLS), 1)
    is_f = (lane >= N_HEADS_M) & (lane < 2 * N_HEADS_M)
    lf = jnp.where(is_f, _log_sigmoid(gall), 0.0)
    row = lax.broadcasted_iota(jnp.int32, (ln, ln), 0)
    col = lax.broadcasted_iota(jnp.int32, (ln, ln), 1)
    causal = col <= row
    tri = jnp.where(causal, 1.0, 0.0).astype(BF16)
    lf_hi, lf_lo = _split_bf16(lf)
    b_all = _dot(tri, lf_hi) + _dot(tri, lf_lo)
    b_all_t = b_all.T
    gall_t = gall.T

    for h in range(N_HEADS_M):
        hs = slice(h * dh, (h + 1) * dh)
        qh = conv_silu(h * dh)
        kh = conv_silu(D_M + h * dh) * (HEAD_DIM_M ** -0.5)
        qb = qh.astype(BF16)
        kb = kh.astype(BF16)
        vb = v_ref[:, hs]

        b_col = b_all[:, N_HEADS_M + h:N_HEADS_M + h + 1]
        li_col = gall[:, h:h + 1]
        b_row = b_all_t[N_HEADS_M + h:N_HEADS_M + h + 1, :]
        li_row = gall_t[h:h + 1, :]
        m_prev = m_ref[h:h + 1, 0:1]

        d_log = jnp.where(causal, (b_col - b_row) + li_row, NEG_BIG)
        inter_log = b_col + m_prev
        m_t = jnp.maximum(jnp.max(d_log, axis=-1, keepdims=True), inter_log)
        scores = _dot_nt(qb, kb) * jnp.exp(d_log - m_t)
        inter_scale = jnp.exp(inter_log - m_t)
        c_prev = c_ref[h]
        n_prev = n_ref[h]
        num = _dot(scores.astype(BF16), vb) + inter_scale * _dot(qb, c_prev.astype(BF16))
        den = (jnp.sum(scores, axis=-1, keepdims=True)
               + inter_scale * jnp.sum(qh * n_prev, axis=-1, keepdims=True))
        hval = num / jnp.maximum(jnp.abs(den), jnp.exp(-m_t))

        g_last = b_col[ln - 1:ln, :]
        w_log = (g_last - b_col) + li_col
        m_new = jnp.maximum(g_last + m_prev, jnp.max(w_log, axis=0, keepdims=True))
        decay = jnp.exp((g_last + m_prev) - m_new)
        kw = kh * jnp.exp(w_log - m_new)
        c_ref[h] = decay * c_prev + _dot_tn(kw.astype(BF16), vb)
        n_ref[h] = decay * n_prev + jnp.sum(kw, axis=0, keepdims=True)
        m_ref[h:h + 1, :] = jnp.broadcast_to(m_new, (1, LANES))

        mu = jnp.mean(hval, axis=-1, keepdims=True)
        hc = hval - mu
        var = jnp.mean(hc * hc, axis=-1, keepdims=True)
        hn = hc * lax.rsqrt(var + LN_EPS) * norm_ref[:, hs]
        y_ref[:, hs] = (hn * _sigmoid(og_ref[:, hs].astype(F32))).astype(BF16)


def _mlstm(proj, gates, b_gates, conv_w, norm_g, bsz, seq):
    ln = MLSTM_CHUNK
    n_chunks = seq // ln
    col_blk = 0
    return pl.pallas_call(
        _mlstm_body,
        grid=(bsz, n_chunks),
        in_specs=[
            pl.BlockSpec((None, ln, D_M), lambda b, c: (b, c, col_blk)),
            pl.BlockSpec((None, ln, D_M), lambda b, c: (b, c, col_blk + 1)),
            pl.BlockSpec((None, ln, D_M), lambda b, c: (b, c, col_blk + 2)),
            pl.BlockSpec((None, ln, D_M), lambda b, c: (b, c, col_blk + 3)),
            pl.BlockSpec((None, ln, GATE_COLS), lambda b, c: (b, c, 0)),
            pl.BlockSpec((1, GATE_COLS), lambda b, c: (0, 0)),
            pl.BlockSpec((CONV_W, 2 * D_M), lambda b, c: (0, 0)),
            pl.BlockSpec((1, D_M), lambda b, c: (0, 0)),
        ],
        out_specs=pl.BlockSpec((None, ln, D_M), lambda b, c: (b, c, 0)),
        out_shape=jax.ShapeDtypeStruct((bsz, seq, D_M), BF16),
        scratch_shapes=[
            pltpu.VMEM((2 * SUBLANES, 2 * D_M), F32),
            pltpu.VMEM(((CONV_W - 1) * ln, ln), BF16),
            pltpu.VMEM((ln, 2 * D_M), F32),
            pltpu.VMEM((N_HEADS_M, HEAD_DIM_M, HEAD_DIM_M), F32),
            pltpu.VMEM((N_HEADS_M, 1, HEAD_DIM_M), F32),
            pltpu.VMEM((SUBLANES, LANES), F32),
        ],
        compiler_params=pltpu.CompilerParams(
            dimension_semantics=("parallel", "arbitrary"), vmem_limit_bytes=VMEM_LIMIT_BYTES),
        name="mlstm",
    )(proj, proj, proj, proj, gates, b_gates, conv_w, norm_g)


def _sb_body(q_ref, k_ref, v_ref, o_ref):
    tq = SB_TQ
    dh = HEAD_DIM_SB
    nh = SB_HEADS_PER_STEP
    qi = pl.program_id(2)
    z_scale = (HEAD_DIM_SB ** -0.5) * LOG2_E
    row = lax.broadcasted_iota(jnp.int32, (nh * tq, tq), 0) & (tq - 1)
    col = lax.broadcasted_iota(jnp.int32, (nh * tq, tq), 1)
    strict = col < row
    row2 = lax.broadcasted_iota(jnp.int32, (2 * tq, tq), 0)
    col2 = lax.broadcasted_iota(jnp.int32, (2 * tq, tq), 1)
    suffix2 = jnp.where((row2 & (tq - 1)) >= col2, 1.0, 0.0).astype(BF16)
    qs = [q_ref[:, hh * dh:(hh + 1) * dh] for hh in range(nh)]

    def block(kb, carry, accs, keep):
        off = pl.multiple_of(kb * tq, tq)
        z2 = jnp.concatenate(
            [_dot_nt(qs[hh], k_ref[pl.ds(off, tq), hh * dh:(hh + 1) * dh]) for hh in range(nh)], axis=0) * z_scale
        neg_part = jnp.minimum(z2, 0.0)
        neg_relu = neg_part - z2
        lom2 = neg_relu - jnp.log2(1.0 + jnp.exp2(neg_part + neg_relu))
        if keep is not None:
            lom2 = jnp.where(keep, lom2, 0.0)
        hi, lo = _split_bf16(lom2)
        rem = _dot(jnp.concatenate([hi, lo], axis=1), suffix2)
        att = jnp.exp2(z2 + (rem + carry))
        if keep is not None:
            att = jnp.where(keep, att, 0.0)
        att = att.astype(BF16)
        accs = tuple(
            accs[hh] + _dot(att[hh * tq:(hh + 1) * tq], v_ref[pl.ds(off, tq), hh * dh:(hh + 1) * dh])
            for hh in range(nh))
        return carry + rem[:, 0:1], accs

    carry, accs = block(qi, jnp.zeros((nh * tq, 1), F32),
                        tuple(jnp.zeros((tq, dh), F32) for _ in range(nh)), strict)
    carry, accs = block(jnp.maximum(qi - 1, 0), carry, accs, qi > 0)

    def cond(loop):
        j, cmax = loop[0], loop[1]
        return jnp.logical_and(j < qi, cmax > SB_SKIP_LOG2)

    def step(loop):
        j = loop[0]
        new_carry, new_accs = block(qi - 1 - j, loop[2], loop[3:], None)
        return (j + 1, jnp.max(new_carry), new_carry) + new_accs

    final = lax.while_loop(cond, step, (jnp.int32(1), jnp.max(carry), carry) + accs)[3:]
    for hh in range(nh):
        o_ref[:, hh * dh:(hh + 1) * dh] = final[hh].astype(BF16)


def _stick_breaking(proj, bsz, seq):
    tq = SB_TQ
    width = SB_HEADS_PER_STEP * HEAD_DIM_SB
    q_blk = PROJ_SB_COL // width
    k_blk = q_blk + D_SB // width
    v_blk = k_blk + D_SB // width
    return pl.pallas_call(
        _sb_body,
        grid=(bsz, D_SB // width, seq // tq),
        in_specs=[
            pl.BlockSpec((None, tq, width), lambda b, h, i: (b, i, q_blk + h)),
            pl.BlockSpec((None, seq, width), lambda b, h, i: (b, 0, k_blk + h)),
            pl.BlockSpec((None, seq, width), lambda b, h, i: (b, 0, v_blk + h)),
        ],
        out_specs=pl.BlockSpec((None, tq, width), lambda b, h, i: (b, i, h)),
        out_shape=jax.ShapeDtypeStruct((bsz, seq, D_SB), BF16),
        compiler_params=pltpu.CompilerParams(
            dimension_semantics=("parallel", "parallel", "arbitrary"), vmem_limit_bytes=VMEM_LIMIT_BYTES),
        name="stickbreak",
    )(proj, proj, proj)


def _merge_ln_body(ym_ref, ysb_ref, ga_lo_ref, ga_hi_ref, gb_lo_ref, gb_hi_ref, x_ref, wum_ref, wus_ref, wo_ref,
                   g_ref, b_ref, o_ref, y_even_ref, y_odd_ref):
    i = pl.program_id(0)
    n_tiles = pl.num_programs(0) - 1

    def ln_of_previous_tile(y_prev_ref):
        o_ref[...] = _layer_norm(y_prev_ref[...], g_ref[...], b_ref[...])

    def tile_step(y_ref, y_prev_ref):
        um = _dot(ym_ref[...], wum_ref[...])
        us = _dot(ysb_ref[...], wus_ref[...])
        ga = jnp.concatenate([ga_lo_ref[...], ga_hi_ref[...]], axis=1).astype(F32)
        gb = jnp.concatenate([gb_lo_ref[...], gb_hi_ref[...]], axis=1).astype(F32)
        merged = _sigmoid(ga) * um + _sigmoid(gb) * us
        y_ref[...] = ALPHA * x_ref[...] + _dot(merged.astype(BF16), wo_ref[...])
        ln_of_previous_tile(y_prev_ref)

    @pl.when(i == 0)
    def _():
        y_odd_ref[...] = jnp.zeros_like(y_odd_ref)

    is_even = i % 2 == 0

    @pl.when(jnp.logical_and(i < n_tiles, is_even))
    def _():
        tile_step(y_even_ref, y_odd_ref)

    @pl.when(jnp.logical_and(i < n_tiles, jnp.logical_not(is_even)))
    def _():
        tile_step(y_odd_ref, y_even_ref)

    @pl.when(jnp.logical_and(i == n_tiles, is_even))
    def _():
        ln_of_previous_tile(y_odd_ref)

    @pl.when(jnp.logical_and(i == n_tiles, jnp.logical_not(is_even)))
    def _():
        ln_of_previous_tile(y_even_ref)


def _merge_ln(ym, ysb, proj, x, w_up_m, w_up_sb, w_out, g, b):
    t = x.shape[0]
    tm = MERGE_TM
    n_tiles = t // tm
    const = dict(pipeline_mode=pl.Buffered(1))
    cur = lambda i: jnp.minimum(i, n_tiles - 1)
    half = D_MODEL // 2
    gate_blk = PROJ_GATE_COL // half
    return pl.pallas_call(
        _merge_ln_body,
        grid=(n_tiles + 1,),
        in_specs=[
            pl.BlockSpec((tm, D_M), lambda i: (cur(i), 0)),
            pl.BlockSpec((tm, D_SB), lambda i: (cur(i), 0)),
            *[pl.BlockSpec((tm, half), lambda i, c=gate_blk + c: (cur(i), c)) for c in range(4)],
            pl.BlockSpec((tm, D_MODEL), lambda i: (cur(i), 0)),
            pl.BlockSpec((D_M, D_MODEL), lambda i: (0, 0), **const),
            pl.BlockSpec((D_SB, D_MODEL), lambda i: (0, 0), **const),
            pl.BlockSpec((D_MODEL, D_MODEL), lambda i: (0, 0), **const),
            pl.BlockSpec((1, D_MODEL), lambda i: (0, 0)),
            pl.BlockSpec((1, D_MODEL), lambda i: (0, 0)),
        ],
        out_specs=pl.BlockSpec((tm, D_MODEL), lambda i: (jnp.maximum(i - 1, 0), 0)),
        out_shape=jax.ShapeDtypeStruct((t, D_MODEL), F32),
        scratch_shapes=[pltpu.VMEM((tm, D_MODEL), F32), pltpu.VMEM((tm, D_MODEL), F32)],
        compiler_params=pltpu.CompilerParams(
            dimension_semantics=("arbitrary",), vmem_limit_bytes=VMEM_LIMIT_BYTES),
        name="merge_ln",
    )(ym, ysb, proj, proj, proj, proj, x, w_up_m, w_up_sb, w_out, g, b)


def _ple_body(x_ref, p_ref, wg_ref, wp_ref, o_ref):
    x = x_ref[...]
    gate = _sigmoid(_dot(x.astype(BF16), wg_ref[...]))
    o_ref[...] = x + gate * _dot(p_ref[...].astype(BF16), wp_ref[...])


def _ple(x, p, w_gate, w_proj):
    t = x.shape[0]
    tm = PLE_TM
    const = dict(pipeline_mode=pl.Buffered(1))
    return pl.pallas_call(
        _ple_body,
        grid=(t // tm,),
        in_specs=[
            pl.BlockSpec((tm, D_MODEL), lambda i: (i, 0)),
            pl.BlockSpec((tm, D_PLE), lambda i: (i, 0)),
            pl.BlockSpec((D_MODEL, D_MODEL), lambda i: (0, 0), **const),
            pl.BlockSpec((D_PLE, D_MODEL), lambda i: (0, 0), **const),
        ],
        out_specs=pl.BlockSpec((tm, D_MODEL), lambda i: (i, 0)),
        out_shape=jax.ShapeDtypeStruct((t, D_MODEL), F32),
        compiler_params=pltpu.CompilerParams(
            dimension_semantics=("parallel",), vmem_limit_bytes=VMEM_LIMIT_BYTES),
        name="ple",
    )(x, p, w_gate, w_proj)


def kernel(x, p, ffn1_w1, ffn1_w3, ffn1_w2, ln1_g, ln1_b, w_in, b_gates_m, conv_m, norm_m, w_up_m, w_up_sb, w_out, ln2_g, ln2_b, ffn2_w1, ffn2_w3, ffn2_w2, ln3_g, ln3_b, w_ple_gate, w_ple_proj):
    bsz, seq, _ = x.shape
    t = bsz * seq
    xf = x.reshape(t, D_MODEL)
    row = lambda v: v.reshape(1, -1)
    for i in range(DEPTH):
        xf, xb = _ffn_ln(xf, ffn1_w1[i].astype(BF16), ffn1_w3[i].astype(BF16), ffn1_w2[i].astype(BF16),
                         row(ln1_g[i]), row(ln1_b[i]), also_bf16=True)

        g0, g1 = 4 * D_M, 4 * D_M + 2 * N_HEADS_M
        w_main = jnp.concatenate([w_in[i][:, :g0].astype(BF16), w_in[i][:, g1:].astype(BF16)], axis=1)
        gate_pad = jnp.zeros((D_MODEL, GATE_COLS - 2 * N_HEADS_M), BF16)
        w_gates = jnp.concatenate([w_in[i][:, g0:g1].astype(BF16), gate_pad], axis=1)
        b_gates = jnp.concatenate([b_gates_m[i], jnp.zeros((GATE_COLS - 2 * N_HEADS_M,), F32)]).reshape(1, GATE_COLS)
        proj, gates = _in_proj(xb, w_main, w_gates)

        proj3 = proj.reshape(bsz, seq, PROJ_COLS)
        ym = _mlstm(proj3, gates.reshape(bsz, seq, GATE_COLS), b_gates, conv_m[i], row(norm_m[i]), bsz, seq)
        ysb = _stick_breaking(proj3, bsz, seq)

        xf = _merge_ln(ym.reshape(t, D_M), ysb.reshape(t, D_SB), proj, xf,
                       w_up_m[i].astype(BF16), w_up_sb[i].astype(BF16), w_out[i].astype(BF16),
                       row(ln2_g[i]), row(ln2_b[i]))
        (xf,) = _ffn_ln(xf, ffn2_w1[i].astype(BF16), ffn2_w3[i].astype(BF16), ffn2_w2[i].astype(BF16),
                        row(ln3_g[i]), row(ln3_b[i]), also_bf16=False)
        xf = _ple(xf, p[i].reshape(t, D_PLE), w_ple_gate[i].astype(BF16), w_ple_proj[i].astype(BF16))
    return xf.reshape(bsz, seq, D_MODEL)
```

```python
import functools

import jax
import jax.numpy as jnp
from jax import lax
from jax.experimental import pallas as pl
from jax.experimental.pallas import tpu as pltpu

D_MODEL = 2048
DEPTH = 1
N_HEADS_M = 4
HEAD_DIM_M = 256
D_M = N_HEADS_M * HEAD_DIM_M
N_HEADS_SB = 8
HEAD_DIM_SB = 128
D_SB = N_HEADS_SB * HEAD_DIM_SB
CONV_W = 4
D_FF = 5632
D_PLE = 256
ALPHA = (2.0 * DEPTH) ** 0.25
LN_EPS = 1e-5
NEG_BIG = -1e30

LANES = 128
SUBLANES = 8
VMEM_LIMIT_BYTES = 56 * 1024 * 1024

PROJ_M_COLS = 4 * D_M
PROJ_S_COLS = 3 * D_SB + 2 * D_MODEL
PROJ_S_GATE_COL = 3 * D_SB
GATE_COLS = LANES

BF16_SUBLANES = 16
FFN_TM = 512
FFN_TF = 512
FFN_LN_ROWS = 48
PROJ_TM = 2048
PROJ_TN = 1024
MLSTM_CHUNK = 256
SB_TQ = 256
SB_HEADS_PER_STEP = 4
LOG2_E = 1.4426950408889634
SB_SKIP_LOG2 = -160.0
MERGE_TM = 256
PLE_TM = 512

F32 = jnp.float32
BF16 = jnp.bfloat16


def _sigmoid(x):
    return 1.0 / (1.0 + jnp.exp(-x))


def _log_sigmoid(x):
    return jnp.minimum(x, 0.0) - jnp.log(1.0 + jnp.exp(-jnp.abs(x)))


def _layer_norm(y, g, b):
    mu = jnp.mean(y, axis=-1, keepdims=True)
    yc = y - mu
    var = jnp.mean(yc * yc, axis=-1, keepdims=True)
    return yc * lax.rsqrt(var + LN_EPS) * g + b


def _split_bf16(x):
    hi = x.astype(BF16)
    lo = (x - hi.astype(F32)).astype(BF16)
    return hi, lo


def _dot(a, b):
    return jnp.dot(a, b, preferred_element_type=F32)


def _dot_nt(a, b):
    return lax.dot_general(a, b, (((1,), (1,)), ((), ())), preferred_element_type=F32)


def _dot_tn(a, b):
    return lax.dot_general(a, b, (((0,), (0,)), ((), ())), preferred_element_type=F32)


def _ffn_ln_body(x_ref, w1_ref, w3_ref, w2_ref, g_ref, b_ref, o_ref, *rest):
    maybe_ob_ref, (xb_ref, acc_even_ref, acc_odd_ref) = rest[:-3], rest[-3:]
    i = pl.program_id(0)
    k = pl.program_id(1)
    n_tiles = pl.num_programs(0) - 1

    def ln_rows_of_previous_tile(acc_prev_ref):
        r0 = pl.multiple_of(jnp.minimum(k * FFN_LN_ROWS, FFN_TM - FFN_LN_ROWS), BF16_SUBLANES)
        rows = pl.ds(r0, FFN_LN_ROWS)
        out = _layer_norm(0.5 * acc_prev_ref[rows, :], g_ref[...], b_ref[...])
        o_ref[rows, :] = out
        for ob_ref in maybe_ob_ref:
            ob_ref[rows, :] = out.astype(BF16)

    def tile_step(acc_ref, acc_prev_ref):
        @pl.when(k == 0)
        def _():
            x = x_ref[...]
            xb_ref[...] = x.astype(BF16)
            acc_ref[...] = (2.0 * ALPHA) * x

        xb = xb_ref[...]
        h1 = _dot(xb, w1_ref[...])
        h3 = _dot(xb, w3_ref[...])
        h = (h1 * _sigmoid(h1)) * h3
        acc_ref[...] += _dot(h.astype(BF16), w2_ref[...])
        ln_rows_of_previous_tile(acc_prev_ref)

    @pl.when(jnp.logical_and(i == 0, k == 0))
    def _():
        acc_odd_ref[...] = jnp.zeros_like(acc_odd_ref)

    is_even = i % 2 == 0

    @pl.when(jnp.logical_and(i < n_tiles, is_even))
    def _():
        tile_step(acc_even_ref, acc_odd_ref)

    @pl.when(jnp.logical_and(i < n_tiles, jnp.logical_not(is_even)))
    def _():
        tile_step(acc_odd_ref, acc_even_ref)

    @pl.when(jnp.logical_and(i == n_tiles, is_even))
    def _():
        ln_rows_of_previous_tile(acc_odd_ref)

    @pl.when(jnp.logical_and(i == n_tiles, jnp.logical_not(is_even)))
    def _():
        ln_rows_of_previous_tile(acc_even_ref)


def _ffn_ln(x, w1, w3, w2, g, b, also_bf16):
    t = x.shape[0]
    n_tiles = t // FFN_TM
    n_k = D_FF // FFN_TF
    assert n_k * FFN_LN_ROWS >= FFN_TM and FFN_LN_ROWS % BF16_SUBLANES == 0
    n_out = 2 if also_bf16 else 1
    out_dtypes = (F32, BF16)[:n_out]
    k_of = lambda i, k: jnp.where(i == n_tiles, n_k - 1, k)
    return pl.pallas_call(
        _ffn_ln_body,
        grid=(n_tiles + 1, n_k),
        in_specs=[
            pl.BlockSpec((FFN_TM, D_MODEL), lambda i, k: (jnp.minimum(i, n_tiles - 1), 0)),
            pl.BlockSpec((D_MODEL, FFN_TF), lambda i, k: (0, k_of(i, k))),
            pl.BlockSpec((D_MODEL, FFN_TF), lambda i, k: (0, k_of(i, k))),
            pl.BlockSpec((FFN_TF, D_MODEL), lambda i, k: (k_of(i, k), 0)),
            pl.BlockSpec((1, D_MODEL), lambda i, k: (0, 0)),
            pl.BlockSpec((1, D_MODEL), lambda i, k: (0, 0)),
        ],
        out_specs=[pl.BlockSpec((FFN_TM, D_MODEL), lambda i, k: (jnp.maximum(i - 1, 0), 0)) for _ in out_dtypes],
        out_shape=[jax.ShapeDtypeStruct((t, D_MODEL), dt) for dt in out_dtypes],
        scratch_shapes=[pltpu.VMEM((FFN_TM, D_MODEL), BF16),
                        pltpu.VMEM((FFN_TM, D_MODEL), F32), pltpu.VMEM((FFN_TM, D_MODEL), F32)],
        compiler_params=pltpu.CompilerParams(
            dimension_semantics=("arbitrary", "arbitrary"), vmem_limit_bytes=VMEM_LIMIT_BYTES),
        name="ffn_ln",
    )(x, w1, w3, w2, g, b)


def _in_proj_body(x_ref, w_ref, *rest):
    if len(rest) == 3:
        wg_ref, o_ref, g_ref = rest

        @pl.when(pl.program_id(1) == 0)
        def _():
            g_ref[...] = _dot(x_ref[...], wg_ref[...])
    else:
        (o_ref,) = rest

    o_ref[...] = _dot(x_ref[...], w_ref[...]).astype(BF16)


def _in_proj(x, w, w_gates=None):
    t, n = x.shape[0], w.shape[1]
    with_gates = w_gates is not None
    in_specs = [
        pl.BlockSpec((PROJ_TM, D_MODEL), lambda i, j: (i, 0)),
        pl.BlockSpec((D_MODEL, PROJ_TN), lambda i, j: (0, j)),
    ]
    out_specs = [pl.BlockSpec((PROJ_TM, PROJ_TN), lambda i, j: (i, j))]
    out_shape = [jax.ShapeDtypeStruct((t, n), BF16)]
    if with_gates:
        in_specs.append(pl.BlockSpec((D_MODEL, GATE_COLS), lambda i, j: (0, 0)))
        out_specs.append(pl.BlockSpec((PROJ_TM, GATE_COLS), lambda i, j: (i, 0)))
        out_shape.append(jax.ShapeDtypeStruct((t, GATE_COLS), F32))
    return pl.pallas_call(
        _in_proj_body,
        grid=(t // PROJ_TM, n // PROJ_TN),
        in_specs=in_specs,
        out_specs=out_specs,
        out_shape=out_shape,
        compiler_params=pltpu.CompilerParams(
            dimension_semantics=("parallel", "arbitrary"), vmem_limit_bytes=VMEM_LIMIT_BYTES),
        name="in_proj",
    )(x, w, *([w_gates] if with_gates else []))


def _mlstm_body(q_ref, k_ref, v_ref, og_ref, g_ref, bg_ref, conv_ref, norm_ref, y_ref,
                tail_ref, shift_ref, qk_ref, c_ref, n_ref, m_ref):
    ln = MLSTM_CHUNK
    dh = HEAD_DIM_M
    taps = CONV_W - 1

    @pl.when(pl.program_id(1) == 0)
    def _():
        tail_ref[0:SUBLANES, :] = jnp.zeros((SUBLANES, 2 * D_M), F32)
        c_ref[...] = jnp.zeros_like(c_ref)
        n_ref[...] = jnp.zeros_like(n_ref)
        m_ref[...] = jnp.full_like(m_ref, NEG_BIG)
        r = lax.broadcasted_iota(jnp.int32, (taps * ln, ln), 0)
        s = lax.broadcasted_iota(jnp.int32, (taps * ln, ln), 1)
        shift_ref[...] = jnp.where(s == (r & (ln - 1)) - (r // ln + 1), 1.0, 0.0).astype(BF16)

    for part, x_ref in enumerate((q_ref, k_ref)):
        for h in range(N_HEADS_M):
            cs = slice(part * D_M + h * dh, part * D_M + (h + 1) * dh)
            x = x_ref[:, h * dh:(h + 1) * dh]
            sh = _dot(shift_ref[...], x)
            acc = sh[(taps - 1) * ln:taps * ln] * conv_ref[0:1, cs]
            for j in range(1, taps):
                acc = acc + sh[(taps - 1 - j) * ln:(taps - j) * ln] * conv_ref[j:j + 1, cs]
            qk_ref[:, cs] = acc + x.astype(F32) * conv_ref[taps:taps + 1, cs]
    tail_ref[SUBLANES:2 * SUBLANES, 0:D_M] = q_ref[0:2 * SUBLANES, :].astype(F32)[0:SUBLANES]
    tail_ref[SUBLANES:2 * SUBLANES, D_M:2 * D_M] = k_ref[0:2 * SUBLANES, :].astype(F32)[0:SUBLANES]
    base = SUBLANES - taps
    first = tail_ref[base:base + SUBLANES, :] * conv_ref[0:1, :]
    for j in range(1, CONV_W):
        first = first + tail_ref[base + j:base + j + SUBLANES, :] * conv_ref[j:j + 1, :]
    qk_ref[0:SUBLANES, :] = first
    tail_ref[0:SUBLANES, 0:D_M] = q_ref[ln - 2 * SUBLANES:ln, :].astype(F32)[SUBLANES:2 * SUBLANES]
    tail_ref[0:SUBLANES, D_M:2 * D_M] = k_ref[ln - 2 * SUBLANES:ln, :].astype(F32)[SUBLANES:2 * SUBLANES]

    def conv_silu(col0):
        acc = qk_ref[:, col0:col0 + dh]
        return acc * _sigmoid(acc)

    gall = g_ref[...] + bg_ref[...]
    lane = lax.broadcasted_iota(jnp.int32, (ln, GATE_COLS), 1)
    is_f = (lane >= N_HEADS_M) & (lane < 2 * N_HEADS_M)
    lf = jnp.where(is_f, _log_sigmoid(gall), 0.0)
    row = lax.broadcasted_iota(jnp.int32, (ln, ln), 0)
    col = lax.broadcasted_iota(jnp.int32, (ln, ln), 1)
    causal = col <= row
    tri = jnp.where(causal, 1.0, 0.0).astype(BF16)
    lf_hi, lf_lo = _split_bf16(lf)
    b_all = _dot(tri, lf_hi) + _dot(tri, lf_lo)
    b_all_t = b_all.T
    gall_t = gall.T

    for h in range(N_HEADS_M):
        hs = slice(h * dh, (h + 1) * dh)
        qh = conv_silu(h * dh)
        kh = conv_silu(D_M + h * dh) * (HEAD_DIM_M ** -0.5)
        qb = qh.astype(BF16)
        kb = kh.astype(BF16)
        vb = v_ref[:, hs]

        b_col = b_all[:, N_HEADS_M + h:N_HEADS_M + h + 1]
        li_col = gall[:, h:h + 1]
        b_row = b_all_t[N_HEADS_M + h:N_HEADS_M + h + 1, :]
        li_row = gall_t[h:h + 1, :]
        m_prev = m_ref[h:h + 1, 0:1]

        d_log = jnp.where(causal, (b_col - b_row) + li_row, NEG_BIG)
        inter_log = b_col + m_prev
        m_t = jnp.maximum(jnp.max(d_log, axis=-1, keepdims=True), inter_log)
        scores = _dot_nt(qb, kb) * jnp.exp(d_log - m_t)
        inter_scale = jnp.exp(inter_log - m_t)
        c_prev = c_ref[h]
        n_prev = n_ref[h]
        num = _dot(scores.astype(BF16), vb) + inter_scale * _dot(qb, c_prev.astype(BF16))
        den = (jnp.sum(scores, axis=-1, keepdims=True)
               + inter_scale * jnp.sum(qh * n_prev, axis=-1, keepdims=True))
        hval = num / jnp.maximum(jnp.abs(den), jnp.exp(-m_t))

        g_last = b_col[ln - 1:ln, :]
        w_log = (g_last - b_col) + li_col
        m_new = jnp.maximum(g_last + m_prev, jnp.max(w_log, axis=0, keepdims=True))
        decay = jnp.exp((g_last + m_prev) - m_new)
        kw = kh * jnp.exp(w_log - m_new)
        c_ref[h] = decay * c_prev + _dot_tn(kw.astype(BF16), vb)
        n_ref[h] = decay * n_prev + jnp.sum(kw, axis=0, keepdims=True)
        m_ref[h:h + 1, :] = jnp.broadcast_to(m_new, (1, LANES))

        mu = jnp.mean(hval, axis=-1, keepdims=True)
        hc = hval - mu
        var = jnp.mean(hc * hc, axis=-1, keepdims=True)
        hn = hc * lax.rsqrt(var + LN_EPS) * norm_ref[:, hs]
        y_ref[:, hs] = (hn * _sigmoid(og_ref[:, hs].astype(F32))).astype(BF16)


def _mlstm(proj, gates, b_gates, conv_w, norm_g, bsz, seq):
    ln = MLSTM_CHUNK
    n_chunks = seq // ln
    col_blk = 0
    return pl.pallas_call(
        _mlstm_body,
        grid=(bsz, n_chunks),
        in_specs=[
            pl.BlockSpec((None, ln, D_M), lambda b, c: (b, c, col_blk)),
            pl.BlockSpec((None, ln, D_M), lambda b, c: (b, c, col_blk + 1)),
            pl.BlockSpec((None, ln, D_M), lambda b, c: (b, c, col_blk + 2)),
            pl.BlockSpec((None, ln, D_M), lambda b, c: (b, c, col_blk + 3)),
            pl.BlockSpec((None, ln, GATE_COLS), lambda b, c: (b, c, 0)),
            pl.BlockSpec((1, GATE_COLS), lambda b, c: (0, 0)),
            pl.BlockSpec((CONV_W, 2 * D_M), lambda b, c: (0, 0)),
            pl.BlockSpec((1, D_M), lambda b, c: (0, 0)),
        ],
        out_specs=pl.BlockSpec((None, ln, D_M), lambda b, c: (b, c, 0)),
        out_shape=jax.ShapeDtypeStruct((bsz, seq, D_M), BF16),
        scratch_shapes=[
            pltpu.VMEM((2 * SUBLANES, 2 * D_M), F32),
            pltpu.VMEM(((CONV_W - 1) * ln, ln), BF16),
            pltpu.VMEM((ln, 2 * D_M), F32),
            pltpu.VMEM((N_HEADS_M, HEAD_DIM_M, HEAD_DIM_M), F32),
            pltpu.VMEM((N_HEADS_M, 1, HEAD_DIM_M), F32),
            pltpu.VMEM((SUBLANES, LANES), F32),
        ],
        compiler_params=pltpu.CompilerParams(
            dimension_semantics=("parallel", "arbitrary"), vmem_limit_bytes=VMEM_LIMIT_BYTES),
        name="mlstm",
    )(proj, proj, proj, proj, gates, b_gates, conv_w, norm_g)


def _sb_body(q_ref, k_ref, v_ref, o_ref):
    tq = SB_TQ
    dh = HEAD_DIM_SB
    nh = SB_HEADS_PER_STEP
    qi = pl.program_id(2)
    z_scale = (HEAD_DIM_SB ** -0.5) * LOG2_E
    row = lax.broadcasted_iota(jnp.int32, (nh * tq, tq), 0) & (tq - 1)
    col = lax.broadcasted_iota(jnp.int32, (nh * tq, tq), 1)
    strict = col < row
    row2 = lax.broadcasted_iota(jnp.int32, (2 * tq, tq), 0)
    col2 = lax.broadcasted_iota(jnp.int32, (2 * tq, tq), 1)
    suffix2 = jnp.where((row2 & (tq - 1)) >= col2, 1.0, 0.0).astype(BF16)
    qs = [q_ref[:, hh * dh:(hh + 1) * dh] for hh in range(nh)]

    def block(kb, carry, accs, keep=None):
        off = pl.multiple_of(kb * tq, tq)
        z2 = jnp.concatenate(
            [_dot_nt(qs[hh], k_ref[pl.ds(off, tq), hh * dh:(hh + 1) * dh]) for hh in range(nh)], axis=0) * z_scale
        if keep is not None:
            z2 = jnp.where(keep, z2, NEG_BIG)
        neg_part = jnp.minimum(z2, 0.0)
        neg_relu = neg_part - z2
        lom2 = neg_relu - jnp.log2(1.0 + jnp.exp2(neg_part + neg_relu))
        hi, lo = _split_bf16(lom2)
        rem = _dot(jnp.concatenate([hi, lo], axis=1), suffix2)
        att = jnp.exp2(z2 + (rem + carry)).astype(BF16)
        accs = tuple(
            accs[hh] + _dot(att[hh * tq:(hh + 1) * tq], v_ref[pl.ds(off, tq), hh * dh:(hh + 1) * dh])
            for hh in range(nh))
        return carry + rem[:, 0:1], accs

    carry, accs = block(qi, jnp.zeros((nh * tq, 1), F32),
                        tuple(jnp.zeros((tq, dh), F32) for _ in range(nh)), strict)
    carry, accs = block(jnp.maximum(qi - 1, 0), jnp.where(qi > 0, carry, NEG_BIG), accs)

    def cond(loop):
        j, cmax = loop[0], loop[1]
        return jnp.logical_and(j < qi, cmax > SB_SKIP_LOG2)

    def step(loop):
        j = loop[0]
        new_carry, new_accs = block(qi - 1 - j, loop[2], loop[3:], None)
        return (j + 1, jnp.max(new_carry), new_carry) + new_accs

    final = lax.while_loop(cond, step, (jnp.int32(1), jnp.max(carry), carry) + accs)[3:]
    for hh in range(nh):
        o_ref[:, hh * dh:(hh + 1) * dh] = final[hh].astype(BF16)


def _stick_breaking(proj, bsz, seq):
    tq = SB_TQ
    width = SB_HEADS_PER_STEP * HEAD_DIM_SB
    q_blk = 0
    k_blk = q_blk + D_SB // width
    v_blk = k_blk + D_SB // width
    return pl.pallas_call(
        _sb_body,
        grid=(bsz, D_SB // width, seq // tq),
        in_specs=[
            pl.BlockSpec((None, tq, width), lambda b, h, i: (b, i, q_blk + h)),
            pl.BlockSpec((None, seq, width), lambda b, h, i: (b, 0, k_blk + h)),
            pl.BlockSpec((None, seq, width), lambda b, h, i: (b, 0, v_blk + h)),
        ],
        out_specs=pl.BlockSpec((None, tq, width), lambda b, h, i: (b, i, h)),
        out_shape=jax.ShapeDtypeStruct((bsz, seq, D_SB), BF16),
        compiler_params=pltpu.CompilerParams(
            dimension_semantics=("parallel", "parallel", "arbitrary"), vmem_limit_bytes=VMEM_LIMIT_BYTES),
        name="stickbreak",
    )(proj, proj, proj)


def _merge_ln_body(ym_ref, ysb_ref, ga_lo_ref, ga_hi_ref, gb_lo_ref, gb_hi_ref, x_ref, wum_ref, wus_ref, wo_ref,
                   g_ref, b_ref, o_ref, y_even_ref, y_odd_ref):
    i = pl.program_id(0)
    n_tiles = pl.num_programs(0) - 1

    def ln_of_previous_tile(y_prev_ref):
        o_ref[...] = _layer_norm(y_prev_ref[...], g_ref[...], b_ref[...])

    def tile_step(y_ref, y_prev_ref):
        um = _dot(ym_ref[...], wum_ref[...])
        us = _dot(ysb_ref[...], wus_ref[...])
        ga = jnp.concatenate([ga_lo_ref[...], ga_hi_ref[...]], axis=1).astype(F32)
        gb = jnp.concatenate([gb_lo_ref[...], gb_hi_ref[...]], axis=1).astype(F32)
        merged = _sigmoid(ga) * um + _sigmoid(gb) * us
        y_ref[...] = ALPHA * x_ref[...] + _dot(merged.astype(BF16), wo_ref[...])
        ln_of_previous_tile(y_prev_ref)

    @pl.when(i == 0)
    def _():
        y_odd_ref[...] = jnp.zeros_like(y_odd_ref)

    is_even = i % 2 == 0

    @pl.when(jnp.logical_and(i < n_tiles, is_even))
    def _():
        tile_step(y_even_ref, y_odd_ref)

    @pl.when(jnp.logical_and(i < n_tiles, jnp.logical_not(is_even)))
    def _():
        tile_step(y_odd_ref, y_even_ref)

    @pl.when(jnp.logical_and(i == n_tiles, is_even))
    def _():
        ln_of_previous_tile(y_odd_ref)

    @pl.when(jnp.logical_and(i == n_tiles, jnp.logical_not(is_even)))
    def _():
        ln_of_previous_tile(y_even_ref)


def _merge_ln(ym, ysb, proj, x, w_up_m, w_up_sb, w_out, g, b):
    t = x.shape[0]
    tm = MERGE_TM
    n_tiles = t // tm
    const = dict(pipeline_mode=pl.Buffered(1))
    cur = lambda i: jnp.minimum(i, n_tiles - 1)
    half = D_MODEL // 2
    gate_blk = PROJ_S_GATE_COL // half
    return pl.pallas_call(
        _merge_ln_body,
        grid=(n_tiles + 1,),
        in_specs=[
            pl.BlockSpec((tm, D_M), lambda i: (cur(i), 0)),
            pl.BlockSpec((tm, D_SB), lambda i: (cur(i), 0)),
            *[pl.BlockSpec((tm, half), lambda i, c=gate_blk + c: (cur(i), c)) for c in range(4)],
            pl.BlockSpec((tm, D_MODEL), lambda i: (cur(i), 0)),
            pl.BlockSpec((D_M, D_MODEL), lambda i: (0, 0), **const),
            pl.BlockSpec((D_SB, D_MODEL), lambda i: (0, 0), **const),
            pl.BlockSpec((D_MODEL, D_MODEL), lambda i: (0, 0), **const),
            pl.BlockSpec((1, D_MODEL), lambda i: (0, 0)),
            pl.BlockSpec((1, D_MODEL), lambda i: (0, 0)),
        ],
        out_specs=pl.BlockSpec((tm, D_MODEL), lambda i: (jnp.maximum(i - 1, 0), 0)),
        out_shape=jax.ShapeDtypeStruct((t, D_MODEL), F32),
        scratch_shapes=[pltpu.VMEM((tm, D_MODEL), F32), pltpu.VMEM((tm, D_MODEL), F32)],
        compiler_params=pltpu.CompilerParams(
            dimension_semantics=("arbitrary",), vmem_limit_bytes=VMEM_LIMIT_BYTES),
        name="merge_ln",
    )(ym, ysb, proj, proj, proj, proj, x, w_up_m, w_up_sb, w_out, g, b)


def _ple_body(x_ref, p_ref, wg_ref, wp_ref, o_ref):
    x = x_ref[...]
    gate = _sigmoid(_dot(x.astype(BF16), wg_ref[...]))
    o_ref[...] = x + gate * _dot(p_ref[...].astype(BF16), wp_ref[...])


def _ple(x, p, w_gate, w_proj):
    t = x.shape[0]
    tm = PLE_TM
    const = dict(pipeline_mode=pl.Buffered(1))
    return pl.pallas_call(
        _ple_body,
        grid=(t // tm,),
        in_specs=[
            pl.BlockSpec((tm, D_MODEL), lambda i: (i, 0)),
            pl.BlockSpec((tm, D_PLE), lambda i: (i, 0)),
            pl.BlockSpec((D_MODEL, D_MODEL), lambda i: (0, 0), **const),
            pl.BlockSpec((D_PLE, D_MODEL), lambda i: (0, 0), **const),
        ],
        out_specs=pl.BlockSpec((tm, D_MODEL), lambda i: (i, 0)),
        out_shape=jax.ShapeDtypeStruct((t, D_MODEL), F32),
        compiler_params=pltpu.CompilerParams(
            dimension_semantics=("parallel",), vmem_limit_bytes=VMEM_LIMIT_BYTES),
        name="ple",
    )(x, p, w_gate, w_proj)


def kernel(x, p, ffn1_w1, ffn1_w3, ffn1_w2, ln1_g, ln1_b, w_in, b_gates_m, conv_m, norm_m, w_up_m, w_up_sb, w_out, ln2_g, ln2_b, ffn2_w1, ffn2_w3, ffn2_w2, ln3_g, ln3_b, w_ple_gate, w_ple_proj):
    bsz, seq, _ = x.shape
    t = bsz * seq
    xf = x.reshape(t, D_MODEL)
    row = lambda v: v.reshape(1, -1)
    for i in range(DEPTH):
        xf, xb = _ffn_ln(xf, ffn1_w1[i].astype(BF16), ffn1_w3[i].astype(BF16), ffn1_w2[i].astype(BF16),
                         row(ln1_g[i]), row(ln1_b[i]), also_bf16=True)

        g0, g1 = PROJ_M_COLS, PROJ_M_COLS + 2 * N_HEADS_M
        gate_pad = jnp.zeros((D_MODEL, GATE_COLS - 2 * N_HEADS_M), BF16)
        w_gates = jnp.concatenate([w_in[i][:, g0:g1].astype(BF16), gate_pad], axis=1)
        b_gates = jnp.concatenate([b_gates_m[i], jnp.zeros((GATE_COLS - 2 * N_HEADS_M,), F32)]).reshape(1, GATE_COLS)
        proj_m, gates = _in_proj(xb, w_in[i][:, :g0].astype(BF16), w_gates)
        (proj_s,) = _in_proj(xb, w_in[i][:, g1:].astype(BF16))

        ym = _mlstm(proj_m.reshape(bsz, seq, PROJ_M_COLS), gates.reshape(bsz, seq, GATE_COLS), b_gates,
                    conv_m[i], row(norm_m[i]), bsz, seq)
        ysb = _stick_breaking(proj_s.reshape(bsz, seq, PROJ_S_COLS), bsz, seq)

        xf = _merge_ln(ym.reshape(t, D_M), ysb.reshape(t, D_SB), proj_s, xf,
                       w_up_m[i].astype(BF16), w_up_sb[i].astype(BF16), w_out[i].astype(BF16),
                       row(ln2_g[i]), row(ln2_b[i]))
        (xf,) = _ffn_ln(xf, ffn2_w1[i].astype(BF16), ffn2_w3[i].astype(BF16), ffn2_w2[i].astype(BF16),
                        row(ln3_g[i]), row(ln3_b[i]), also_bf16=False)
        xf = _ple(xf, p[i].reshape(t, D_PLE), w_ple_gate[i].astype(BF16), w_ple_proj[i].astype(BF16))
    return xf.reshape(bsz, seq, D_MODEL)
```

```python
import functools

import jax
import jax.numpy as jnp
from jax import lax
from jax.experimental import pallas as pl
from jax.experimental.pallas import tpu as pltpu

D_MODEL = 2048
DEPTH = 1
N_HEADS_M = 4
HEAD_DIM_M = 256
D_M = N_HEADS_M * HEAD_DIM_M
N_HEADS_SB = 8
HEAD_DIM_SB = 128
D_SB = N_HEADS_SB * HEAD_DIM_SB
CONV_W = 4
D_FF = 5632
D_PLE = 256
ALPHA = (2.0 * DEPTH) ** 0.25
LN_EPS = 1e-5
NEG_BIG = -1e30

LANES = 128
SUBLANES = 8
VMEM_LIMIT_BYTES = 56 * 1024 * 1024

PROJ_M_COLS = 4 * D_M
PROJ_S_COLS = 3 * D_SB + 2 * D_MODEL
PROJ_S_GATE_COL = 3 * D_SB
GATE_COLS = LANES

BF16_SUBLANES = 16
FFN_TM = 512
FFN_TF = 512
FFN_LN_ROWS = 48
PROJ_TM = 2048
PROJ_TN = 1024
MLSTM_CHUNK = 256
SB_TQ = 256
SB_HEADS_PER_STEP = 4
LOG2_E = 1.4426950408889634
SB_SKIP_LOG2 = -160.0
MERGE_TM = 256
PLE_TM = 512

F32 = jnp.float32
BF16 = jnp.bfloat16


def _sigmoid(x):
    return 1.0 / (1.0 + jnp.exp(-x))


def _log_sigmoid(x):
    return jnp.minimum(x, 0.0) - jnp.log(1.0 + jnp.exp(-jnp.abs(x)))


def _layer_norm(y, g, b):
    mu = jnp.mean(y, axis=-1, keepdims=True)
    yc = y - mu
    var = jnp.mean(yc * yc, axis=-1, keepdims=True)
    return yc * lax.rsqrt(var + LN_EPS) * g + b


def _split_bf16(x):
    hi = x.astype(BF16)
    lo = (x - hi.astype(F32)).astype(BF16)
    return hi, lo


def _dot(a, b):
    return jnp.dot(a, b, preferred_element_type=F32)


def _dot_nt(a, b):
    return lax.dot_general(a, b, (((1,), (1,)), ((), ())), preferred_element_type=F32)


def _dot_tn(a, b):
    return lax.dot_general(a, b, (((0,), (0,)), ((), ())), preferred_element_type=F32)


def _ffn_ln_body(x_ref, w13_ref, w2_ref, g_ref, b_ref, o_ref, *rest):
    maybe_ob_ref, (xb_ref, acc_even_ref, acc_odd_ref) = rest[:-3], rest[-3:]
    i = pl.program_id(0)
    k = pl.program_id(1)
    n_tiles = pl.num_programs(0) - 1

    def ln_rows_of_previous_tile(acc_prev_ref):
        r0 = pl.multiple_of(jnp.minimum(k * FFN_LN_ROWS, FFN_TM - FFN_LN_ROWS), BF16_SUBLANES)
        rows = pl.ds(r0, FFN_LN_ROWS)
        out = _layer_norm(0.5 * acc_prev_ref[rows, :], g_ref[...], b_ref[...])
        o_ref[rows, :] = out
        for ob_ref in maybe_ob_ref:
            ob_ref[rows, :] = out.astype(BF16)

    def tile_step(acc_ref, acc_prev_ref):
        @pl.when(k == 0)
        def _():
            x = x_ref[...]
            xb_ref[...] = x.astype(BF16)
            acc_ref[...] = (2.0 * ALPHA) * x

        xb = xb_ref[...]
        h13 = _dot(xb, w13_ref[...])
        h1, h3 = h13[:, :FFN_TF], h13[:, FFN_TF:]
        h = (h1 * _sigmoid(h1)) * h3
        acc_ref[...] += _dot(h.astype(BF16), w2_ref[...])
        ln_rows_of_previous_tile(acc_prev_ref)

    @pl.when(jnp.logical_and(i == 0, k == 0))
    def _():
        acc_odd_ref[...] = jnp.zeros_like(acc_odd_ref)

    is_even = i % 2 == 0

    @pl.when(jnp.logical_and(i < n_tiles, is_even))
    def _():
        tile_step(acc_even_ref, acc_odd_ref)

    @pl.when(jnp.logical_and(i < n_tiles, jnp.logical_not(is_even)))
    def _():
        tile_step(acc_odd_ref, acc_even_ref)

    @pl.when(jnp.logical_and(i == n_tiles, is_even))
    def _():
        ln_rows_of_previous_tile(acc_odd_ref)

    @pl.when(jnp.logical_and(i == n_tiles, jnp.logical_not(is_even)))
    def _():
        ln_rows_of_previous_tile(acc_even_ref)


def _ffn_ln(x, w1, w3, w2, g, b, also_bf16):
    t = x.shape[0]
    n_tiles = t // FFN_TM
    n_k = D_FF // FFN_TF
    w13 = jnp.stack([w1.astype(BF16).reshape(D_MODEL, n_k, FFN_TF), w3.astype(BF16).reshape(D_MODEL, n_k, FFN_TF)],
                    axis=2).reshape(D_MODEL, 2 * D_FF)
    w2 = w2.astype(BF16)
    assert n_k * FFN_LN_ROWS >= FFN_TM and FFN_LN_ROWS % BF16_SUBLANES == 0
    n_out = 2 if also_bf16 else 1
    out_dtypes = (F32, BF16)[:n_out]
    k_of = lambda i, k: jnp.where(i == n_tiles, n_k - 1, k)
    return pl.pallas_call(
        _ffn_ln_body,
        grid=(n_tiles + 1, n_k),
        in_specs=[
            pl.BlockSpec((FFN_TM, D_MODEL), lambda i, k: (jnp.minimum(i, n_tiles - 1), 0)),
            pl.BlockSpec((D_MODEL, 2 * FFN_TF), lambda i, k: (0, k_of(i, k))),
            pl.BlockSpec((FFN_TF, D_MODEL), lambda i, k: (k_of(i, k), 0)),
            pl.BlockSpec((1, D_MODEL), lambda i, k: (0, 0)),
            pl.BlockSpec((1, D_MODEL), lambda i, k: (0, 0)),
        ],
        out_specs=[pl.BlockSpec((FFN_TM, D_MODEL), lambda i, k: (jnp.maximum(i - 1, 0), 0)) for _ in out_dtypes],
        out_shape=[jax.ShapeDtypeStruct((t, D_MODEL), dt) for dt in out_dtypes],
        scratch_shapes=[pltpu.VMEM((FFN_TM, D_MODEL), BF16),
                        pltpu.VMEM((FFN_TM, D_MODEL), F32), pltpu.VMEM((FFN_TM, D_MODEL), F32)],
        compiler_params=pltpu.CompilerParams(
            dimension_semantics=("arbitrary", "arbitrary"), vmem_limit_bytes=VMEM_LIMIT_BYTES),
        name="ffn_ln",
    )(x, w13, w2, g, b)


def _in_proj_body(x_ref, w_ref, o_ref):
    o_ref[...] = _dot(x_ref[...], w_ref[...]).astype(BF16)


def _in_proj(x, w):
    t, n = x.shape[0], w.shape[1]
    return pl.pallas_call(
        _in_proj_body,
        grid=(t // PROJ_TM, n // PROJ_TN),
        in_specs=[
            pl.BlockSpec((PROJ_TM, D_MODEL), lambda i, j: (i, 0)),
            pl.BlockSpec((D_MODEL, PROJ_TN), lambda i, j: (0, j)),
        ],
        out_specs=pl.BlockSpec((PROJ_TM, PROJ_TN), lambda i, j: (i, j)),
        out_shape=jax.ShapeDtypeStruct((t, n), BF16),
        compiler_params=pltpu.CompilerParams(
            dimension_semantics=("parallel", "arbitrary"), vmem_limit_bytes=VMEM_LIMIT_BYTES),
        name="in_proj",
    )(x, w)


def _mlstm_body(q_ref, k_ref, v_ref, og_ref, xin_ref, wg_ref, bg_ref, conv_ref, norm_ref, y_ref,
                tail_ref, shift_ref, qk_ref, c_ref, n_ref, m_ref):
    ln = MLSTM_CHUNK
    dh = HEAD_DIM_M
    taps = CONV_W - 1

    @pl.when(pl.program_id(1) == 0)
    def _():
        tail_ref[0:SUBLANES, :] = jnp.zeros((SUBLANES, 2 * D_M), F32)
        c_ref[...] = jnp.zeros_like(c_ref)
        n_ref[...] = jnp.zeros_like(n_ref)
        m_ref[...] = jnp.full_like(m_ref, NEG_BIG)
        r = lax.broadcasted_iota(jnp.int32, (taps * ln, ln), 0)
        s = lax.broadcasted_iota(jnp.int32, (taps * ln, ln), 1)
        shift_ref[...] = jnp.where(s == (r & (ln - 1)) - (r // ln + 1), 1.0, 0.0).astype(BF16)

    for part, x_ref in enumerate((q_ref, k_ref)):
        for h in range(N_HEADS_M):
            cs = slice(part * D_M + h * dh, part * D_M + (h + 1) * dh)
            x = x_ref[:, h * dh:(h + 1) * dh]
            sh = _dot(shift_ref[...], x)
            acc = sh[(taps - 1) * ln:taps * ln] * conv_ref[0:1, cs]
            for j in range(1, taps):
                acc = acc + sh[(taps - 1 - j) * ln:(taps - j) * ln] * conv_ref[j:j + 1, cs]
            qk_ref[:, cs] = acc + x.astype(F32) * conv_ref[taps:taps + 1, cs]
    tail_ref[SUBLANES:2 * SUBLANES, 0:D_M] = q_ref[0:2 * SUBLANES, :].astype(F32)[0:SUBLANES]
    tail_ref[SUBLANES:2 * SUBLANES, D_M:2 * D_M] = k_ref[0:2 * SUBLANES, :].astype(F32)[0:SUBLANES]
    base = SUBLANES - taps
    first = tail_ref[base:base + SUBLANES, :] * conv_ref[0:1, :]
    for j in range(1, CONV_W):
        first = first + tail_ref[base + j:base + j + SUBLANES, :] * conv_ref[j:j + 1, :]
    qk_ref[0:SUBLANES, :] = first
    tail_ref[0:SUBLANES, 0:D_M] = q_ref[ln - 2 * SUBLANES:ln, :].astype(F32)[SUBLANES:2 * SUBLANES]
    tail_ref[0:SUBLANES, D_M:2 * D_M] = k_ref[ln - 2 * SUBLANES:ln, :].astype(F32)[SUBLANES:2 * SUBLANES]

    def conv_silu(col0):
        acc = qk_ref[:, col0:col0 + dh]
        return acc * _sigmoid(acc)

    gall = _dot(xin_ref[...], wg_ref[...]) + bg_ref[...]
    lane = lax.broadcasted_iota(jnp.int32, (ln, GATE_COLS), 1)
    is_f = (lane >= N_HEADS_M) & (lane < 2 * N_HEADS_M)
    lf = jnp.where(is_f, _log_sigmoid(gall), 0.0)
    row = lax.broadcasted_iota(jnp.int32, (ln, ln), 0)
    col = lax.broadcasted_iota(jnp.int32, (ln, ln), 1)
    causal = col <= row
    tri = jnp.where(causal, 1.0, 0.0).astype(BF16)
    lf_hi, lf_lo = _split_bf16(lf)
    b_all = _dot(tri, lf_hi) + _dot(tri, lf_lo)
    b_all_t = b_all.T
    gall_t = gall.T

    for h in range(N_HEADS_M):
        hs = slice(h * dh, (h + 1) * dh)
        qh = conv_silu(h * dh)
        kh = conv_silu(D_M + h * dh) * (HEAD_DIM_M ** -0.5)
        qb = qh.astype(BF16)
        kb = kh.astype(BF16)
        vb = v_ref[:, hs]

        b_col = b_all[:, N_HEADS_M + h:N_HEADS_M + h + 1]
        li_col = gall[:, h:h + 1]
        b_row = b_all_t[N_HEADS_M + h:N_HEADS_M + h + 1, :]
        li_row = gall_t[h:h + 1, :]
        m_prev = m_ref[h:h + 1, 0:1]

        d_log = jnp.where(causal, (b_col - b_row) + li_row, NEG_BIG)
        inter_log = b_col + m_prev
        m_t = jnp.maximum(jnp.max(d_log, axis=-1, keepdims=True), inter_log)
        scores = _dot_nt(qb, kb) * jnp.exp(d_log - m_t)
        inter_scale = jnp.exp(inter_log - m_t)
        c_prev = c_ref[h]
        n_prev = n_ref[h]
        num = _dot(scores.astype(BF16), vb) + inter_scale * _dot(qb, c_prev.astype(BF16))
        den = (jnp.sum(scores, axis=-1, keepdims=True)
               + inter_scale * jnp.sum(qh * n_prev, axis=-1, keepdims=True))
        hval = num / jnp.maximum(jnp.abs(den), jnp.exp(-m_t))

        g_last = b_col[ln - 1:ln, :]
        w_log = (g_last - b_col) + li_col
        m_new = jnp.maximum(g_last + m_prev, jnp.max(w_log, axis=0, keepdims=True))
        decay = jnp.exp((g_last + m_prev) - m_new)
        kw = kh * jnp.exp(w_log - m_new)
        c_ref[h] = decay * c_prev + _dot_tn(kw.astype(BF16), vb)
        n_ref[h] = decay * n_prev + jnp.sum(kw, axis=0, keepdims=True)
        m_ref[h:h + 1, :] = jnp.broadcast_to(m_new, (1, LANES))

        mu = jnp.mean(hval, axis=-1, keepdims=True)
        hc = hval - mu
        var = jnp.mean(hc * hc, axis=-1, keepdims=True)
        hn = hc * lax.rsqrt(var + LN_EPS) * norm_ref[:, hs]
        y_ref[:, hs] = (hn * _sigmoid(og_ref[:, hs].astype(F32))).astype(BF16)


def _mlstm(proj, x, w_gates, b_gates, conv_w, norm_g, bsz, seq):
    ln = MLSTM_CHUNK
    n_chunks = seq // ln
    col_blk = 0
    return pl.pallas_call(
        _mlstm_body,
        grid=(bsz, n_chunks),
        in_specs=[
            pl.BlockSpec((None, ln, D_M), lambda b, c: (b, c, col_blk)),
            pl.BlockSpec((None, ln, D_M), lambda b, c: (b, c, col_blk + 1)),
            pl.BlockSpec((None, ln, D_M), lambda b, c: (b, c, col_blk + 2)),
            pl.BlockSpec((None, ln, D_M), lambda b, c: (b, c, col_blk + 3)),
            pl.BlockSpec((None, ln, D_MODEL), lambda b, c: (b, c, 0)),
            pl.BlockSpec((D_MODEL, GATE_COLS), lambda b, c: (0, 0)),
            pl.BlockSpec((1, GATE_COLS), lambda b, c: (0, 0)),
            pl.BlockSpec((CONV_W, 2 * D_M), lambda b, c: (0, 0)),
            pl.BlockSpec((1, D_M), lambda b, c: (0, 0)),
        ],
        out_specs=pl.BlockSpec((None, ln, D_M), lambda b, c: (b, c, 0)),
        out_shape=jax.ShapeDtypeStruct((bsz, seq, D_M), BF16),
        scratch_shapes=[
            pltpu.VMEM((2 * SUBLANES, 2 * D_M), F32),
            pltpu.VMEM(((CONV_W - 1) * ln, ln), BF16),
            pltpu.VMEM((ln, 2 * D_M), F32),
            pltpu.VMEM((N_HEADS_M, HEAD_DIM_M, HEAD_DIM_M), F32),
            pltpu.VMEM((N_HEADS_M, 1, HEAD_DIM_M), F32),
            pltpu.VMEM((SUBLANES, LANES), F32),
        ],
        compiler_params=pltpu.CompilerParams(
            dimension_semantics=("parallel", "arbitrary"), vmem_limit_bytes=VMEM_LIMIT_BYTES),
        name="mlstm",
    )(proj, proj, proj, proj, x, w_gates, b_gates, conv_w, norm_g)


def _sb_body(q_ref, k_ref, v_ref, o_ref):
    tq = SB_TQ
    dh = HEAD_DIM_SB
    nh = SB_HEADS_PER_STEP
    qi = pl.program_id(2)
    z_scale = (HEAD_DIM_SB ** -0.5) * LOG2_E
    row = lax.broadcasted_iota(jnp.int32, (nh * tq, tq), 0) & (tq - 1)
    col = lax.broadcasted_iota(jnp.int32, (nh * tq, tq), 1)
    strict = col < row
    row2 = lax.broadcasted_iota(jnp.int32, (2 * tq, tq), 0)
    col2 = lax.broadcasted_iota(jnp.int32, (2 * tq, tq), 1)
    suffix2 = jnp.where((row2 & (tq - 1)) >= col2, 1.0, 0.0).astype(BF16)
    qs = [q_ref[:, hh * dh:(hh + 1) * dh] for hh in range(nh)]

    def block(kb, carry, accs, keep=None):
        off = pl.multiple_of(kb * tq, tq)
        z2 = jnp.concatenate(
            [_dot_nt(qs[hh], k_ref[pl.ds(off, tq), hh * dh:(hh + 1) * dh]) for hh in range(nh)], axis=0) * z_scale
        if keep is not None:
            z2 = jnp.where(keep, z2, NEG_BIG)
        neg_part = jnp.minimum(z2, 0.0)
        neg_relu = neg_part - z2
        lom2 = neg_relu - jnp.log2(1.0 + jnp.exp2(neg_part + neg_relu))
        hi, lo = _split_bf16(lom2)
        rem = _dot(jnp.concatenate([hi, lo], axis=1), suffix2)
        att = jnp.exp2(z2 + (rem + carry)).astype(BF16)
        accs = tuple(
            accs[hh] + _dot(att[hh * tq:(hh + 1) * tq], v_ref[pl.ds(off, tq), hh * dh:(hh + 1) * dh])
            for hh in range(nh))
        return carry + rem[:, 0:1], accs

    carry, accs = block(qi, jnp.zeros((nh * tq, 1), F32),
                        tuple(jnp.zeros((tq, dh), F32) for _ in range(nh)), strict)
    carry, accs = block(jnp.maximum(qi - 1, 0), jnp.where(qi > 0, carry, NEG_BIG), accs)

    def cond(loop):
        j, cmax = loop[0], loop[1]
        return jnp.logical_and(j < qi, cmax > SB_SKIP_LOG2)

    def step(loop):
        j = loop[0]
        new_carry, new_accs = block(qi - 1 - j, loop[2], loop[3:], None)
        return (j + 1, jnp.max(new_carry), new_carry) + new_accs

    final = lax.while_loop(cond, step, (jnp.int32(1), jnp.max(carry), carry) + accs)[3:]
    for hh in range(nh):
        o_ref[:, hh * dh:(hh + 1) * dh] = final[hh].astype(BF16)


def _stick_breaking(proj, bsz, seq):
    tq = SB_TQ
    width = SB_HEADS_PER_STEP * HEAD_DIM_SB
    q_blk = 0
    k_blk = q_blk + D_SB // width
    v_blk = k_blk + D_SB // width
    return pl.pallas_call(
        _sb_body,
        grid=(bsz, D_SB // width, seq // tq),
        in_specs=[
            pl.BlockSpec((None, tq, width), lambda b, h, i: (b, i, q_blk + h)),
            pl.BlockSpec((None, seq, width), lambda b, h, i: (b, 0, k_blk + h)),
            pl.BlockSpec((None, seq, width), lambda b, h, i: (b, 0, v_blk + h)),
        ],
        out_specs=pl.BlockSpec((None, tq, width), lambda b, h, i: (b, i, h)),
        out_shape=jax.ShapeDtypeStruct((bsz, seq, D_SB), BF16),
        compiler_params=pltpu.CompilerParams(
            dimension_semantics=("parallel", "parallel", "arbitrary"), vmem_limit_bytes=VMEM_LIMIT_BYTES),
        name="stickbreak",
    )(proj, proj, proj)


def _merge_ln_body(ym_ref, ysb_ref, ga_lo_ref, ga_hi_ref, gb_lo_ref, gb_hi_ref, x_ref, wum_ref, wus_ref, wo_ref,
                   g_ref, b_ref, o_ref, y_even_ref, y_odd_ref):
    i = pl.program_id(0)
    n_tiles = pl.num_programs(0) - 1

    def ln_of_previous_tile(y_prev_ref):
        o_ref[...] = _layer_norm(y_prev_ref[...], g_ref[...], b_ref[...])

    def tile_step(y_ref, y_prev_ref):
        um = _dot(ym_ref[...], wum_ref[...])
        us = _dot(ysb_ref[...], wus_ref[...])
        ga = jnp.concatenate([ga_lo_ref[...], ga_hi_ref[...]], axis=1).astype(F32)
        gb = jnp.concatenate([gb_lo_ref[...], gb_hi_ref[...]], axis=1).astype(F32)
        merged = _sigmoid(ga) * um + _sigmoid(gb) * us
        y_ref[...] = ALPHA * x_ref[...] + _dot(merged.astype(BF16), wo_ref[...])
        ln_of_previous_tile(y_prev_ref)

    @pl.when(i == 0)
    def _():
        y_odd_ref[...] = jnp.zeros_like(y_odd_ref)

    is_even = i % 2 == 0

    @pl.when(jnp.logical_and(i < n_tiles, is_even))
    def _():
        tile_step(y_even_ref, y_odd_ref)

    @pl.when(jnp.logical_and(i < n_tiles, jnp.logical_not(is_even)))
    def _():
        tile_step(y_odd_ref, y_even_ref)

    @pl.when(jnp.logical_and(i == n_tiles, is_even))
    def _():
        ln_of_previous_tile(y_odd_ref)

    @pl.when(jnp.logical_and(i == n_tiles, jnp.logical_not(is_even)))
    def _():
        ln_of_previous_tile(y_even_ref)


def _merge_ln(ym, ysb, proj, x, w_up_m, w_up_sb, w_out, g, b):
    t = x.shape[0]
    tm = MERGE_TM
    n_tiles = t // tm
    const = dict(pipeline_mode=pl.Buffered(1))
    cur = lambda i: jnp.minimum(i, n_tiles - 1)
    half = D_MODEL // 2
    gate_blk = PROJ_S_GATE_COL // half
    return pl.pallas_call(
        _merge_ln_body,
        grid=(n_tiles + 1,),
        in_specs=[
            pl.BlockSpec((tm, D_M), lambda i: (cur(i), 0)),
            pl.BlockSpec((tm, D_SB), lambda i: (cur(i), 0)),
            *[pl.BlockSpec((tm, half), lambda i, c=gate_blk + c: (cur(i), c)) for c in range(4)],
            pl.BlockSpec((tm, D_MODEL), lambda i: (cur(i), 0)),
            pl.BlockSpec((D_M, D_MODEL), lambda i: (0, 0), **const),
            pl.BlockSpec((D_SB, D_MODEL), lambda i: (0, 0), **const),
            pl.BlockSpec((D_MODEL, D_MODEL), lambda i: (0, 0), **const),
            pl.BlockSpec((1, D_MODEL), lambda i: (0, 0)),
            pl.BlockSpec((1, D_MODEL), lambda i: (0, 0)),
        ],
        out_specs=pl.BlockSpec((tm, D_MODEL), lambda i: (jnp.maximum(i - 1, 0), 0)),
        out_shape=jax.ShapeDtypeStruct((t, D_MODEL), F32),
        scratch_shapes=[pltpu.VMEM((tm, D_MODEL), F32), pltpu.VMEM((tm, D_MODEL), F32)],
        compiler_params=pltpu.CompilerParams(
            dimension_semantics=("arbitrary",), vmem_limit_bytes=VMEM_LIMIT_BYTES),
        name="merge_ln",
    )(ym, ysb, proj, proj, proj, proj, x, w_up_m, w_up_sb, w_out, g, b)


def _ple_body(x_ref, p_ref, wg_ref, wp_ref, o_ref):
    x = x_ref[...]
    gate = _sigmoid(_dot(x.astype(BF16), wg_ref[...]))
    o_ref[...] = x + gate * _dot(p_ref[...].astype(BF16), wp_ref[...])


def _ple(x, p, w_gate, w_proj):
    t = x.shape[0]
    tm = PLE_TM
    const = dict(pipeline_mode=pl.Buffered(1))
    return pl.pallas_call(
        _ple_body,
        grid=(t // tm,),
        in_specs=[
            pl.BlockSpec((tm, D_MODEL), lambda i: (i, 0)),
            pl.BlockSpec((tm, D_PLE), lambda i: (i, 0)),
            pl.BlockSpec((D_MODEL, D_MODEL), lambda i: (0, 0), **const),
            pl.BlockSpec((D_PLE, D_MODEL), lambda i: (0, 0), **const),
        ],
        out_specs=pl.BlockSpec((tm, D_MODEL), lambda i: (i, 0)),
        out_shape=jax.ShapeDtypeStruct((t, D_MODEL), F32),
        compiler_params=pltpu.CompilerParams(
            dimension_semantics=("parallel",), vmem_limit_bytes=VMEM_LIMIT_BYTES),
        name="ple",
    )(x, p, w_gate, w_proj)


def kernel(x, p, ffn1_w1, ffn1_w3, ffn1_w2, ln1_g, ln1_b, w_in, b_gates_m, conv_m, norm_m, w_up_m, w_up_sb, w_out, ln2_g, ln2_b, ffn2_w1, ffn2_w3, ffn2_w2, ln3_g, ln3_b, w_ple_gate, w_ple_proj):
    bsz, seq, _ = x.shape
    t = bsz * seq
    xf = x.reshape(t, D_MODEL)
    row = lambda v: v.reshape(1, -1)
    for i in range(DEPTH):
        xf, xb = _ffn_ln(xf, ffn1_w1[i], ffn1_w3[i], ffn1_w2[i], row(ln1_g[i]), row(ln1_b[i]), also_bf16=True)

        g0, g1 = PROJ_M_COLS, PROJ_M_COLS + 2 * N_HEADS_M
        gate_pad = jnp.zeros((D_MODEL, GATE_COLS - 2 * N_HEADS_M), BF16)
        w_gates = jnp.concatenate([w_in[i][:, g0:g1].astype(BF16), gate_pad], axis=1)
        b_gates = jnp.concatenate([b_gates_m[i], jnp.zeros((GATE_COLS - 2 * N_HEADS_M,), F32)]).reshape(1, GATE_COLS)
        proj_m = _in_proj(xb, w_in[i][:, :g0].astype(BF16))
        proj_s = _in_proj(xb, w_in[i][:, g1:].astype(BF16))

        ym = _mlstm(proj_m.reshape(bsz, seq, PROJ_M_COLS), xb.reshape(bsz, seq, D_MODEL), w_gates, b_gates,
                    conv_m[i], row(norm_m[i]), bsz, seq)
        ysb = _stick_breaking(proj_s.reshape(bsz, seq, PROJ_S_COLS), bsz, seq)

        xf = _merge_ln(ym.reshape(t, D_M), ysb.reshape(t, D_SB), proj_s, xf,
                       w_up_m[i].astype(BF16), w_up_sb[i].astype(BF16), w_out[i].astype(BF16),
                       row(ln2_g[i]), row(ln2_b[i]))
        (xf,) = _ffn_ln(xf, ffn2_w1[i], ffn2_w3[i], ffn2_w2[i], row(ln3_g[i]), row(ln3_b[i]), also_bf16=False)
        xf = _ple(xf, p[i].reshape(t, D_PLE), w_ple_gate[i].astype(BF16), w_ple_proj[i].astype(BF16))
    return xf.reshape(bsz, seq, D_MODEL)
```

```python
import jax
import jax.numpy as jnp
from jax import lax
from jax.experimental import pallas as pl
from jax.experimental.pallas import tpu as pltpu

D_MODEL = 2048
DEPTH = 1
N_HEADS_M = 4
HEAD_DIM_M = 256
D_M = N_HEADS_M * HEAD_DIM_M
N_HEADS_SB = 8
HEAD_DIM_SB = 128
D_SB = N_HEADS_SB * HEAD_DIM_SB
CONV_W = 4
D_FF = 5632
D_PLE = 256
ALPHA = (2.0 * DEPTH) ** 0.25
LN_EPS = 1e-5
NEG_BIG = -1e30

LANES = 128
SUBLANES = 8
VMEM_LIMIT_BYTES = 56 * 1024 * 1024

PROJ_M_COLS = 4 * D_M
PROJ_S_COLS = 3 * D_SB + 2 * D_MODEL
PROJ_S_GATE_COL = 3 * D_SB
GATE_COLS = LANES

BF16_SUBLANES = 16
FFN_TM = 512
FFN_TF = 512
FFN_LN_ROWS = 48
PROJ_TM = 2048
PROJ_TN = 1024
MLSTM_CHUNK = 256
SB_TQ = 256
SB_HEADS_PER_STEP = 4
LOG2_E = 1.4426950408889634
SB_SKIP_LOG2 = -160.0
MERGE_TM = 256
PLE_TM = 512

F32 = jnp.float32
BF16 = jnp.bfloat16


def _sigmoid(x):
    return 1.0 / (1.0 + jnp.exp(-x))


def _log_sigmoid(x):
    return jnp.minimum(x, 0.0) - jnp.log(1.0 + jnp.exp(-jnp.abs(x)))


def _layer_norm(y, g, b):
    mu = jnp.mean(y, axis=-1, keepdims=True)
    yc = y - mu
    var = jnp.mean(yc * yc, axis=-1, keepdims=True)
    return yc * lax.rsqrt(var + LN_EPS) * g + b


def _split_bf16(x):
    hi = x.astype(BF16)
    lo = (x - hi.astype(F32)).astype(BF16)
    return hi, lo


def _dot(a, b):
    return jnp.dot(a, b, preferred_element_type=F32)


def _dot_nt(a, b):
    return lax.dot_general(a, b, (((1,), (1,)), ((), ())), preferred_element_type=F32)


def _dot_tn(a, b):
    return lax.dot_general(a, b, (((0,), (0,)), ((), ())), preferred_element_type=F32)


def _ffn_ln_body(x_ref, w1_ref, w3_ref, w2_ref, g_ref, b_ref, o_ref, *rest):
    maybe_ob_ref, (xb_ref, acc_even_ref, acc_odd_ref) = rest[:-3], rest[-3:]
    i = pl.program_id(0)
    k = pl.program_id(1)
    n_tiles = pl.num_programs(0) - 1

    def ln_rows_of_previous_tile(acc_prev_ref):
        r0 = pl.multiple_of(jnp.minimum(k * FFN_LN_ROWS, FFN_TM - FFN_LN_ROWS), BF16_SUBLANES)
        rows = pl.ds(r0, FFN_LN_ROWS)
        out = _layer_norm(0.5 * acc_prev_ref[rows, :], g_ref[...], b_ref[...])
        o_ref[rows, :] = out
        for ob_ref in maybe_ob_ref:
            ob_ref[rows, :] = out.astype(BF16)

    def tile_step(acc_ref, acc_prev_ref):
        @pl.when(k == 0)
        def _():
            x = x_ref[...]
            xb_ref[...] = x.astype(BF16)
            acc_ref[...] = (2.0 * ALPHA) * x

        xb = xb_ref[...]
        h1 = _dot(xb, w1_ref[...])
        h3 = _dot(xb, w3_ref[...])
        h = (h1 * _sigmoid(h1)) * h3
        acc_ref[...] += _dot(h.astype(BF16), w2_ref[...])
        ln_rows_of_previous_tile(acc_prev_ref)

    @pl.when(jnp.logical_and(i == 0, k == 0))
    def _():
        acc_odd_ref[...] = jnp.zeros_like(acc_odd_ref)

    is_even = i % 2 == 0

    @pl.when(jnp.logical_and(i < n_tiles, is_even))
    def _():
        tile_step(acc_even_ref, acc_odd_ref)

    @pl.when(jnp.logical_and(i < n_tiles, jnp.logical_not(is_even)))
    def _():
        tile_step(acc_odd_ref, acc_even_ref)

    @pl.when(jnp.logical_and(i == n_tiles, is_even))
    def _():
        ln_rows_of_previous_tile(acc_odd_ref)

    @pl.when(jnp.logical_and(i == n_tiles, jnp.logical_not(is_even)))
    def _():
        ln_rows_of_previous_tile(acc_even_ref)


def _ffn_ln(x, w1, w3, w2, g, b, also_bf16):
    t = x.shape[0]
    n_tiles = t // FFN_TM
    n_k = D_FF // FFN_TF
    assert n_k * FFN_LN_ROWS >= FFN_TM and FFN_LN_ROWS % BF16_SUBLANES == 0
    n_out = 2 if also_bf16 else 1
    out_dtypes = (F32, BF16)[:n_out]
    k_of = lambda i, k: jnp.where(i == n_tiles, n_k - 1, k)
    return pl.pallas_call(
        _ffn_ln_body,
        grid=(n_tiles + 1, n_k),
        in_specs=[
            pl.BlockSpec((FFN_TM, D_MODEL), lambda i, k: (jnp.minimum(i, n_tiles - 1), 0)),
            pl.BlockSpec((D_MODEL, FFN_TF), lambda i, k: (0, k_of(i, k))),
            pl.BlockSpec((D_MODEL, FFN_TF), lambda i, k: (0, k_of(i, k))),
            pl.BlockSpec((FFN_TF, D_MODEL), lambda i, k: (k_of(i, k), 0)),
            pl.BlockSpec((1, D_MODEL), lambda i, k: (0, 0)),
            pl.BlockSpec((1, D_MODEL), lambda i, k: (0, 0)),
        ],
        out_specs=[pl.BlockSpec((FFN_TM, D_MODEL), lambda i, k: (jnp.maximum(i - 1, 0), 0)) for _ in out_dtypes],
        out_shape=[jax.ShapeDtypeStruct((t, D_MODEL), dt) for dt in out_dtypes],
        scratch_shapes=[pltpu.VMEM((FFN_TM, D_MODEL), BF16),
                        pltpu.VMEM((FFN_TM, D_MODEL), F32), pltpu.VMEM((FFN_TM, D_MODEL), F32)],
        compiler_params=pltpu.CompilerParams(
            dimension_semantics=("arbitrary", "arbitrary"), vmem_limit_bytes=VMEM_LIMIT_BYTES),
        name="ffn_ln",
    )(x, w1, w3, w2, g, b)


def _in_proj_body(x_ref, w_ref, *rest):
    if len(rest) == 3:
        wg_ref, o_ref, g_ref = rest

        @pl.when(pl.program_id(1) == 0)
        def _():
            g_ref[...] = _dot(x_ref[...], wg_ref[...])
    else:
        (o_ref,) = rest

    o_ref[...] = _dot(x_ref[...], w_ref[...]).astype(BF16)


def _in_proj(x, w, w_gates=None):
    t, n = x.shape[0], w.shape[1]
    with_gates = w_gates is not None
    in_specs = [
        pl.BlockSpec((PROJ_TM, D_MODEL), lambda i, j: (i, 0)),
        pl.BlockSpec((D_MODEL, PROJ_TN), lambda i, j: (0, j)),
    ]
    out_specs = [pl.BlockSpec((PROJ_TM, PROJ_TN), lambda i, j: (i, j))]
    out_shape = [jax.ShapeDtypeStruct((t, n), BF16)]
    if with_gates:
        in_specs.append(pl.BlockSpec((D_MODEL, GATE_COLS), lambda i, j: (0, 0)))
        out_specs.append(pl.BlockSpec((PROJ_TM, GATE_COLS), lambda i, j: (i, 0)))
        out_shape.append(jax.ShapeDtypeStruct((t, GATE_COLS), F32))
    return pl.pallas_call(
        _in_proj_body,
        grid=(t // PROJ_TM, n // PROJ_TN),
        in_specs=in_specs,
        out_specs=out_specs,
        out_shape=out_shape,
        compiler_params=pltpu.CompilerParams(
            dimension_semantics=("parallel", "arbitrary"), vmem_limit_bytes=VMEM_LIMIT_BYTES),
        name="in_proj",
    )(x, w, *([w_gates] if with_gates else []))


def _mlstm_body(q_ref, k_ref, v_ref, og_ref, g_ref, bg_ref, conv_ref, norm_ref, y_ref,
                tail_ref, shift_ref, qk_ref, c_ref, n_ref, m_ref):
    ln = MLSTM_CHUNK
    dh = HEAD_DIM_M
    taps = CONV_W - 1

    @pl.when(pl.program_id(1) == 0)
    def _():
        tail_ref[0:SUBLANES, :] = jnp.zeros((SUBLANES, 2 * D_M), F32)
        c_ref[...] = jnp.zeros_like(c_ref)
        n_ref[...] = jnp.zeros_like(n_ref)
        m_ref[...] = jnp.full_like(m_ref, NEG_BIG)
        r = lax.broadcasted_iota(jnp.int32, (taps * ln, ln), 0)
        s = lax.broadcasted_iota(jnp.int32, (taps * ln, ln), 1)
        shift_ref[...] = jnp.where(s == (r & (ln - 1)) - (r // ln + 1), 1.0, 0.0).astype(BF16)

    for part, x_ref in enumerate((q_ref, k_ref)):
        for h in range(N_HEADS_M):
            cs = slice(part * D_M + h * dh, part * D_M + (h + 1) * dh)
            x = x_ref[:, h * dh:(h + 1) * dh]
            sh = _dot(shift_ref[...], x)
            acc = sh[(taps - 1) * ln:taps * ln] * conv_ref[0:1, cs]
            for j in range(1, taps):
                acc = acc + sh[(taps - 1 - j) * ln:(taps - j) * ln] * conv_ref[j:j + 1, cs]
            qk_ref[:, cs] = acc + x.astype(F32) * conv_ref[taps:taps + 1, cs]
    tail_ref[SUBLANES:2 * SUBLANES, 0:D_M] = q_ref[0:2 * SUBLANES, :].astype(F32)[0:SUBLANES]
    tail_ref[SUBLANES:2 * SUBLANES, D_M:2 * D_M] = k_ref[0:2 * SUBLANES, :].astype(F32)[0:SUBLANES]
    base = SUBLANES - taps
    first = tail_ref[base:base + SUBLANES, :] * conv_ref[0:1, :]
    for j in range(1, CONV_W):
        first = first + tail_ref[base + j:base + j + SUBLANES, :] * conv_ref[j:j + 1, :]
    qk_ref[0:SUBLANES, :] = first
    tail_ref[0:SUBLANES, 0:D_M] = q_ref[ln - 2 * SUBLANES:ln, :].astype(F32)[SUBLANES:2 * SUBLANES]
    tail_ref[0:SUBLANES, D_M:2 * D_M] = k_ref[ln - 2 * SUBLANES:ln, :].astype(F32)[SUBLANES:2 * SUBLANES]

    def conv_silu(col0):
        acc = qk_ref[:, col0:col0 + dh]
        return acc * _sigmoid(acc)

    gall = g_ref[...] + bg_ref[...]
    lane = lax.broadcasted_iota(jnp.int32, (ln, GATE_COLS), 1)
    is_f = (lane >= N_HEADS_M) & (lane < 2 * N_HEADS_M)
    lf = jnp.where(is_f, _log_sigmoid(gall), 0.0)
    row = lax.broadcasted_iota(jnp.int32, (ln, ln), 0)
    col = lax.broadcasted_iota(jnp.int32, (ln, ln), 1)
    causal = col <= row
    tri = jnp.where(causal, 1.0, 0.0).astype(BF16)
    lf_hi, lf_lo = _split_bf16(lf)
    b_all = _dot(tri, lf_hi) + _dot(tri, lf_lo)
    b_all_t = b_all.T
    gall_t = gall.T

    for h in range(N_HEADS_M):
        hs = slice(h * dh, (h + 1) * dh)
        qh = conv_silu(h * dh)
        kh = conv_silu(D_M + h * dh) * (HEAD_DIM_M ** -0.5)
        qb = qh.astype(BF16)
        kb = kh.astype(BF16)
        vb = v_ref[:, hs]

        b_col = b_all[:, N_HEADS_M + h:N_HEADS_M + h + 1]
        li_col = gall[:, h:h + 1]
        b_row = b_all_t[N_HEADS_M + h:N_HEADS_M + h + 1, :]
        li_row = gall_t[h:h + 1, :]
        m_prev = m_ref[h:h + 1, 0:1]

        d_log = jnp.where(causal, (b_col - b_row) + li_row, NEG_BIG)
        inter_log = b_col + m_prev
        m_t = jnp.maximum(jnp.max(d_log, axis=-1, keepdims=True), inter_log)
        scores = _dot_nt(qb, kb) * jnp.exp(d_log - m_t)
        inter_scale = jnp.exp(inter_log - m_t)
        c_prev = c_ref[h]
        n_prev = n_ref[h]
        num = _dot(scores.astype(BF16), vb) + inter_scale * _dot(qb, c_prev.astype(BF16))
        den = (jnp.sum(scores, axis=-1, keepdims=True)
               + inter_scale * jnp.sum(qh * n_prev, axis=-1, keepdims=True))
        hval = num / jnp.maximum(jnp.abs(den), jnp.exp(-m_t))

        g_last = b_col[ln - 1:ln, :]
        w_log = (g_last - b_col) + li_col
        m_new = jnp.maximum(g_last + m_prev, jnp.max(w_log, axis=0, keepdims=True))
        decay = jnp.exp((g_last + m_prev) - m_new)
        kw = kh * jnp.exp(w_log - m_new)
        c_ref[h] = decay * c_prev + _dot_tn(kw.astype(BF16), vb)
        n_ref[h] = decay * n_prev + jnp.sum(kw, axis=0, keepdims=True)
        m_ref[h:h + 1, :] = jnp.broadcast_to(m_new, (1, LANES))

        mu = jnp.mean(hval, axis=-1, keepdims=True)
        hc = hval - mu
        var = jnp.mean(hc * hc, axis=-1, keepdims=True)
        hn = hc * lax.rsqrt(var + LN_EPS) * norm_ref[:, hs]
        y_ref[:, hs] = (hn * _sigmoid(og_ref[:, hs].astype(F32))).astype(BF16)


def _mlstm(proj, gates, b_gates, conv_w, norm_g, bsz, seq):
    ln = MLSTM_CHUNK
    n_chunks = seq // ln
    col_blk = 0
    return pl.pallas_call(
        _mlstm_body,
        grid=(bsz, n_chunks),
        in_specs=[
            pl.BlockSpec((None, ln, D_M), lambda b, c: (b, c, col_blk)),
            pl.BlockSpec((None, ln, D_M), lambda b, c: (b, c, col_blk + 1)),
            pl.BlockSpec((None, ln, D_M), lambda b, c: (b, c, col_blk + 2)),
            pl.BlockSpec((None, ln, D_M), lambda b, c: (b, c, col_blk + 3)),
            pl.BlockSpec((None, ln, GATE_COLS), lambda b, c: (b, c, 0)),
            pl.BlockSpec((1, GATE_COLS), lambda b, c: (0, 0)),
            pl.BlockSpec((CONV_W, 2 * D_M), lambda b, c: (0, 0)),
            pl.BlockSpec((1, D_M), lambda b, c: (0, 0)),
        ],
        out_specs=pl.BlockSpec((None, ln, D_M), lambda b, c: (b, c, 0)),
        out_shape=jax.ShapeDtypeStruct((bsz, seq, D_M), BF16),
        scratch_shapes=[
            pltpu.VMEM((2 * SUBLANES, 2 * D_M), F32),
            pltpu.VMEM(((CONV_W - 1) * ln, ln), BF16),
            pltpu.VMEM((ln, 2 * D_M), F32),
            pltpu.VMEM((N_HEADS_M, HEAD_DIM_M, HEAD_DIM_M), F32),
            pltpu.VMEM((N_HEADS_M, 1, HEAD_DIM_M), F32),
            pltpu.VMEM((SUBLANES, LANES), F32),
        ],
        compiler_params=pltpu.CompilerParams(
            dimension_semantics=("parallel", "arbitrary"), vmem_limit_bytes=VMEM_LIMIT_BYTES),
        name="mlstm",
    )(proj, proj, proj, proj, gates, b_gates, conv_w, norm_g)


def _sb_body(q_ref, k_ref, v_ref, o_ref):
    tq = SB_TQ
    dh = HEAD_DIM_SB
    nh = SB_HEADS_PER_STEP
    qi = pl.program_id(2)
    z_scale = (HEAD_DIM_SB ** -0.5) * LOG2_E
    row = lax.broadcasted_iota(jnp.int32, (nh * tq, tq), 0) & (tq - 1)
    col = lax.broadcasted_iota(jnp.int32, (nh * tq, tq), 1)
    strict = col < row
    row2 = lax.broadcasted_iota(jnp.int32, (2 * tq, tq), 0)
    col2 = lax.broadcasted_iota(jnp.int32, (2 * tq, tq), 1)
    suffix2 = jnp.where((row2 & (tq - 1)) >= col2, 1.0, 0.0).astype(BF16)
    qs = [q_ref[:, hh * dh:(hh + 1) * dh] for hh in range(nh)]

    def block(kb, carry, accs, keep=None):
        off = pl.multiple_of(kb * tq, tq)
        z2 = jnp.concatenate(
            [_dot_nt(qs[hh], k_ref[pl.ds(off, tq), hh * dh:(hh + 1) * dh]) for hh in range(nh)], axis=0) * z_scale
        if keep is not None:
            z2 = jnp.where(keep, z2, NEG_BIG)
        neg_part = jnp.minimum(z2, 0.0)
        neg_relu = neg_part - z2
        lom2 = neg_relu - jnp.log2(1.0 + jnp.exp2(neg_part + neg_relu))
        hi, lo = _split_bf16(lom2)
        rem = _dot(jnp.concatenate([hi, lo], axis=1), suffix2)
        att = jnp.exp2(z2 + (rem + carry)).astype(BF16)
        accs = tuple(
            accs[hh] + _dot(att[hh * tq:(hh + 1) * tq], v_ref[pl.ds(off, tq), hh * dh:(hh + 1) * dh])
            for hh in range(nh))
        return carry + rem[:, 0:1], accs

    carry, accs = block(qi, jnp.zeros((nh * tq, 1), F32),
                        tuple(jnp.zeros((tq, dh), F32) for _ in range(nh)), strict)
    carry, accs = block(jnp.maximum(qi - 1, 0), jnp.where(qi > 0, carry, NEG_BIG), accs)

    def cond(loop):
        j, cmax = loop[0], loop[1]
        return jnp.logical_and(j < qi, cmax > SB_SKIP_LOG2)

    def step(loop):
        j = loop[0]
        new_carry, new_accs = block(qi - 1 - j, loop[2], loop[3:], None)
        return (j + 1, jnp.max(new_carry), new_carry) + new_accs

    final = lax.while_loop(cond, step, (jnp.int32(1), jnp.max(carry), carry) + accs)[3:]
    for hh in range(nh):
        o_ref[:, hh * dh:(hh + 1) * dh] = final[hh].astype(BF16)


def _stick_breaking(proj, bsz, seq):
    tq = SB_TQ
    width = SB_HEADS_PER_STEP * HEAD_DIM_SB
    q_blk = 0
    k_blk = q_blk + D_SB // width
    v_blk = k_blk + D_SB // width
    return pl.pallas_call(
        _sb_body,
        grid=(bsz, D_SB // width, seq // tq),
        in_specs=[
            pl.BlockSpec((None, tq, width), lambda b, h, i: (b, i, q_blk + h)),
            pl.BlockSpec((None, seq, width), lambda b, h, i: (b, 0, k_blk + h)),
            pl.BlockSpec((None, seq, width), lambda b, h, i: (b, 0, v_blk + h)),
        ],
        out_specs=pl.BlockSpec((None, tq, width), lambda b, h, i: (b, i, h)),
        out_shape=jax.ShapeDtypeStruct((bsz, seq, D_SB), BF16),
        compiler_params=pltpu.CompilerParams(
            dimension_semantics=("parallel", "parallel", "arbitrary"), vmem_limit_bytes=VMEM_LIMIT_BYTES),
        name="stickbreak",
    )(proj, proj, proj)


def _merge_ln_body(ym_ref, ysb_ref, ga_lo_ref, ga_hi_ref, gb_lo_ref, gb_hi_ref, x_ref, wum_ref, wus_ref, wo_ref,
                   g_ref, b_ref, o_ref, y_even_ref, y_odd_ref):
    i = pl.program_id(0)
    n_tiles = pl.num_programs(0) - 1

    def ln_of_previous_tile(y_prev_ref):
        o_ref[...] = _layer_norm(y_prev_ref[...], g_ref[...], b_ref[...])

    def tile_step(y_ref, y_prev_ref):
        um = _dot(ym_ref[...], wum_ref[...])
        us = _dot(ysb_ref[...], wus_ref[...])
        ga = jnp.concatenate([ga_lo_ref[...], ga_hi_ref[...]], axis=1).astype(F32)
        gb = jnp.concatenate([gb_lo_ref[...], gb_hi_ref[...]], axis=1).astype(F32)
        merged = _sigmoid(ga) * um + _sigmoid(gb) * us
        y_ref[...] = ALPHA * x_ref[...] + _dot(merged.astype(BF16), wo_ref[...])
        ln_of_previous_tile(y_prev_ref)

    @pl.when(i == 0)
    def _():
        y_odd_ref[...] = jnp.zeros_like(y_odd_ref)

    is_even = i % 2 == 0

    @pl.when(jnp.logical_and(i < n_tiles, is_even))
    def _():
        tile_step(y_even_ref, y_odd_ref)

    @pl.when(jnp.logical_and(i < n_tiles, jnp.logical_not(is_even)))
    def _():
        tile_step(y_odd_ref, y_even_ref)

    @pl.when(jnp.logical_and(i == n_tiles, is_even))
    def _():
        ln_of_previous_tile(y_odd_ref)

    @pl.when(jnp.logical_and(i == n_tiles, jnp.logical_not(is_even)))
    def _():
        ln_of_previous_tile(y_even_ref)


def _merge_ln(ym, ysb, proj, x, w_up_m, w_up_sb, w_out, g, b):
    t = x.shape[0]
    tm = MERGE_TM
    n_tiles = t // tm
    const = dict(pipeline_mode=pl.Buffered(1))
    cur = lambda i: jnp.minimum(i, n_tiles - 1)
    half = D_MODEL // 2
    gate_blk = PROJ_S_GATE_COL // half
    return pl.pallas_call(
        _merge_ln_body,
        grid=(n_tiles + 1,),
        in_specs=[
            pl.BlockSpec((tm, D_M), lambda i: (cur(i), 0)),
            pl.BlockSpec((tm, D_SB), lambda i: (cur(i), 0)),
            *[pl.BlockSpec((tm, half), lambda i, c=gate_blk + c: (cur(i), c)) for c in range(4)],
            pl.BlockSpec((tm, D_MODEL), lambda i: (cur(i), 0)),
            pl.BlockSpec((D_M, D_MODEL), lambda i: (0, 0), **const),
            pl.BlockSpec((D_SB, D_MODEL), lambda i: (0, 0), **const),
            pl.BlockSpec((D_MODEL, D_MODEL), lambda i: (0, 0), **const),
            pl.BlockSpec((1, D_MODEL), lambda i: (0, 0)),
            pl.BlockSpec((1, D_MODEL), lambda i: (0, 0)),
        ],
        out_specs=pl.BlockSpec((tm, D_MODEL), lambda i: (jnp.maximum(i - 1, 0), 0)),
        out_shape=jax.ShapeDtypeStruct((t, D_MODEL), F32),
        scratch_shapes=[pltpu.VMEM((tm, D_MODEL), F32), pltpu.VMEM((tm, D_MODEL), F32)],
        compiler_params=pltpu.CompilerParams(
            dimension_semantics=("arbitrary",), vmem_limit_bytes=VMEM_LIMIT_BYTES),
        name="merge_ln",
    )(ym, ysb, proj, proj, proj, proj, x, w_up_m, w_up_sb, w_out, g, b)


def _ple_body(x_ref, p_ref, wg_ref, wp_ref, o_ref):
    x = x_ref[...]
    gate = _sigmoid(_dot(x.astype(BF16), wg_ref[...]))
    o_ref[...] = x + gate * _dot(p_ref[...].astype(BF16), wp_ref[...])


def _ple(x, p, w_gate, w_proj):
    t = x.shape[0]
    tm = PLE_TM
    const = dict(pipeline_mode=pl.Buffered(1))
    return pl.pallas_call(
        _ple_body,
        grid=(t // tm,),
        in_specs=[
            pl.BlockSpec((tm, D_MODEL), lambda i: (i, 0)),
            pl.BlockSpec((tm, D_PLE), lambda i: (i, 0)),
            pl.BlockSpec((D_MODEL, D_MODEL), lambda i: (0, 0), **const),
            pl.BlockSpec((D_PLE, D_MODEL), lambda i: (0, 0), **const),
        ],
        out_specs=pl.BlockSpec((tm, D_MODEL), lambda i: (i, 0)),
        out_shape=jax.ShapeDtypeStruct((t, D_MODEL), F32),
        compiler_params=pltpu.CompilerParams(
            dimension_semantics=("parallel",), vmem_limit_bytes=VMEM_LIMIT_BYTES),
        name="ple",
    )(x, p, w_gate, w_proj)


def kernel(x, p, ffn1_w1, ffn1_w3, ffn1_w2, ln1_g, ln1_b, w_in, b_gates_m, conv_m, norm_m, w_up_m, w_up_sb, w_out, ln2_g, ln2_b, ffn2_w1, ffn2_w3, ffn2_w2, ln3_g, ln3_b, w_ple_gate, w_ple_proj):
    bsz, seq, _ = x.shape
    t = bsz * seq
    xf = x.reshape(t, D_MODEL)
    row = lambda v: v.reshape(1, -1)
    for i in range(DEPTH):
        xf, xb = _ffn_ln(xf, ffn1_w1[i].astype(BF16), ffn1_w3[i].astype(BF16), ffn1_w2[i].astype(BF16),
                         row(ln1_g[i]), row(ln1_b[i]), also_bf16=True)

        g0, g1 = PROJ_M_COLS, PROJ_M_COLS + 2 * N_HEADS_M
        gate_pad = jnp.zeros((D_MODEL, GATE_COLS - 2 * N_HEADS_M), BF16)
        w_gates = jnp.concatenate([w_in[i][:, g0:g1].astype(BF16), gate_pad], axis=1)
        b_gates = jnp.concatenate([b_gates_m[i], jnp.zeros((GATE_COLS - 2 * N_HEADS_M,), F32)]).reshape(1, GATE_COLS)
        proj_m, gates = _in_proj(xb, w_in[i][:, :g0].astype(BF16), w_gates)
        (proj_s,) = _in_proj(xb, w_in[i][:, g1:].astype(BF16))

        ym = _mlstm(proj_m.reshape(bsz, seq, PROJ_M_COLS), gates.reshape(bsz, seq, GATE_COLS), b_gates,
                    conv_m[i], row(norm_m[i]), bsz, seq)
        ysb = _stick_breaking(proj_s.reshape(bsz, seq, PROJ_S_COLS), bsz, seq)

        xf = _merge_ln(ym.reshape(t, D_M), ysb.reshape(t, D_SB), proj_s, xf,
                       w_up_m[i].astype(BF16), w_up_sb[i].astype(BF16), w_out[i].astype(BF16),
                       row(ln2_g[i]), row(ln2_b[i]))
        (xf,) = _ffn_ln(xf, ffn2_w1[i].astype(BF16), ffn2_w3[i].astype(BF16), ffn2_w2[i].astype(BF16),
                        row(ln3_g[i]), row(ln3_b[i]), also_bf16=False)
        xf = _ple(xf, p[i].reshape(t, D_PLE), w_ple_gate[i].astype(BF16), w_ple_proj[i].astype(BF16))
    return xf.reshape(bsz, seq, D_MODEL)
```

```python
import jax
import jax.numpy as jnp
from jax import lax
from jax.experimental import pallas as pl
from jax.experimental.pallas import tpu as pltpu

D_MODEL = 2048
DEPTH = 1
N_HEADS_M = 4
HEAD_DIM_M = 256
D_M = N_HEADS_M * HEAD_DIM_M
N_HEADS_SB = 8
HEAD_DIM_SB = 128
D_SB = N_HEADS_SB * HEAD_DIM_SB
CONV_W = 4
D_FF = 5632
D_PLE = 256
ALPHA = (2.0 * DEPTH) ** 0.25
LN_EPS = 1e-5
NEG_BIG = -1e30

LANES = 128
SUBLANES = 8
VMEM_LIMIT_BYTES = 56 * 1024 * 1024

PROJ_M_COLS = 4 * D_M
PROJ_S_COLS = 3 * D_SB + 2 * D_MODEL
PROJ_S_GATE_COL = 3 * D_SB
GATE_COLS = LANES

BF16_SUBLANES = 16
FFN_TM = 512
FFN_TF = 512
FFN_LN_ROWS = 48
PROJ_TM = 2048
PROJ_TN = 1024
MLSTM_CHUNK = 256
SB_TQ = 256
SB_HEADS_PER_STEP = 4
LOG2_E = 1.4426950408889634
SB_SKIP_LOG2 = -160.0
MERGE_TM = 256
PLE_TM = 512

F32 = jnp.float32
BF16 = jnp.bfloat16


def _sigmoid(x):
    return 1.0 / (1.0 + jnp.exp(-x))


def _log_sigmoid(x):
    return jnp.minimum(x, 0.0) - jnp.log(1.0 + jnp.exp(-jnp.abs(x)))


def _layer_norm(y, g, b):
    mu = jnp.mean(y, axis=-1, keepdims=True)
    yc = y - mu
    var = jnp.mean(yc * yc, axis=-1, keepdims=True)
    return yc * lax.rsqrt(var + LN_EPS) * g + b


def _split_bf16(x):
    hi = x.astype(BF16)
    lo = (x - hi.astype(F32)).astype(BF16)
    return hi, lo


def _dot(a, b):
    return jnp.dot(a, b, preferred_element_type=F32)


def _dot_nt(a, b):
    return lax.dot_general(a, b, (((1,), (1,)), ((), ())), preferred_element_type=F32)


def _dot_tn(a, b):
    return lax.dot_general(a, b, (((0,), (0,)), ((), ())), preferred_element_type=F32)


def _ffn_ln_body(x_ref, w1_ref, w3_ref, w2_ref, g_ref, b_ref, o_ref, *rest):
    maybe_ob_ref, (xb_ref, acc_even_ref, acc_odd_ref) = rest[:-3], rest[-3:]
    i = pl.program_id(0)
    k = pl.program_id(1)
    n_tiles = pl.num_programs(0) - 1

    def ln_rows_of_previous_tile(acc_prev_ref):
        r0 = pl.multiple_of(jnp.minimum(k * FFN_LN_ROWS, FFN_TM - FFN_LN_ROWS), BF16_SUBLANES)
        rows = pl.ds(r0, FFN_LN_ROWS)
        out = _layer_norm(0.5 * acc_prev_ref[rows, :], g_ref[...], b_ref[...])
        o_ref[rows, :] = out
        for ob_ref in maybe_ob_ref:
            ob_ref[rows, :] = out.astype(BF16)

    def tile_step(acc_ref, acc_prev_ref):
        @pl.when(k == 0)
        def _():
            x = x_ref[...]
            xb_ref[...] = x.astype(BF16)
            acc_ref[...] = (2.0 * ALPHA) * x

        xb = xb_ref[...]
        h1 = _dot(xb, w1_ref[...])
        h3 = _dot(xb, w3_ref[...])
        h = (h1 * _sigmoid(h1)) * h3
        acc_ref[...] += _dot(h.astype(BF16), w2_ref[...])
        ln_rows_of_previous_tile(acc_prev_ref)

    @pl.when(jnp.logical_and(i == 0, k == 0))
    def _():
        acc_odd_ref[...] = jnp.zeros_like(acc_odd_ref)

    is_even = i % 2 == 0

    @pl.when(jnp.logical_and(i < n_tiles, is_even))
    def _():
        tile_step(acc_even_ref, acc_odd_ref)

    @pl.when(jnp.logical_and(i < n_tiles, jnp.logical_not(is_even)))
    def _():
        tile_step(acc_odd_ref, acc_even_ref)

    @pl.when(jnp.logical_and(i == n_tiles, is_even))
    def _():
        ln_rows_of_previous_tile(acc_odd_ref)

    @pl.when(jnp.logical_and(i == n_tiles, jnp.logical_not(is_even)))
    def _():
        ln_rows_of_previous_tile(acc_even_ref)


def _ffn_ln(x, w1, w3, w2, g, b, also_bf16):
    t = x.shape[0]
    n_tiles = t // FFN_TM
    n_k = D_FF // FFN_TF
    assert n_k * FFN_LN_ROWS >= FFN_TM and FFN_LN_ROWS % BF16_SUBLANES == 0
    n_out = 2 if also_bf16 else 1
    out_dtypes = (F32, BF16)[:n_out]
    k_of = lambda i, k: jnp.where(i == n_tiles, n_k - 1, k)
    return pl.pallas_call(
        _ffn_ln_body,
        grid=(n_tiles + 1, n_k),
        in_specs=[
            pl.BlockSpec((FFN_TM, D_MODEL), lambda i, k: (jnp.minimum(i, n_tiles - 1), 0)),
            pl.BlockSpec((D_MODEL, FFN_TF), lambda i, k: (0, k_of(i, k))),
            pl.BlockSpec((D_MODEL, FFN_TF), lambda i, k: (0, k_of(i, k))),
            pl.BlockSpec((FFN_TF, D_MODEL), lambda i, k: (k_of(i, k), 0)),
            pl.BlockSpec((1, D_MODEL), lambda i, k: (0, 0)),
            pl.BlockSpec((1, D_MODEL), lambda i, k: (0, 0)),
        ],
        out_specs=[pl.BlockSpec((FFN_TM, D_MODEL), lambda i, k: (jnp.maximum(i - 1, 0), 0)) for _ in out_dtypes],
        out_shape=[jax.ShapeDtypeStruct((t, D_MODEL), dt) for dt in out_dtypes],
        scratch_shapes=[pltpu.VMEM((FFN_TM, D_MODEL), BF16),
                        pltpu.VMEM((FFN_TM, D_MODEL), F32), pltpu.VMEM((FFN_TM, D_MODEL), F32)],
        compiler_params=pltpu.CompilerParams(
            dimension_semantics=("arbitrary", "arbitrary"), vmem_limit_bytes=VMEM_LIMIT_BYTES),
        name="ffn_ln",
    )(x, w1, w3, w2, g, b)


def _in_proj_body(x_ref, w_ref, *rest):
    if len(rest) == 3:
        wg_ref, o_ref, g_ref = rest

        @pl.when(pl.program_id(1) == 0)
        def _():
            g_ref[...] = _dot(x_ref[...], wg_ref[...])
    else:
        (o_ref,) = rest

    o_ref[...] = _dot(x_ref[...], w_ref[...]).astype(BF16)


def _in_proj(x, w, w_gates=None):
    t, n = x.shape[0], w.shape[1]
    with_gates = w_gates is not None
    in_specs = [
        pl.BlockSpec((PROJ_TM, D_MODEL), lambda i, j: (i, 0)),
        pl.BlockSpec((D_MODEL, PROJ_TN), lambda i, j: (0, j)),
    ]
    out_specs = [pl.BlockSpec((PROJ_TM, PROJ_TN), lambda i, j: (i, j))]
    out_shape = [jax.ShapeDtypeStruct((t, n), BF16)]
    if with_gates:
        in_specs.append(pl.BlockSpec((D_MODEL, GATE_COLS), lambda i, j: (0, 0)))
        out_specs.append(pl.BlockSpec((PROJ_TM, GATE_COLS), lambda i, j: (i, 0)))
        out_shape.append(jax.ShapeDtypeStruct((t, GATE_COLS), F32))
    return pl.pallas_call(
        _in_proj_body,
        grid=(t // PROJ_TM, n // PROJ_TN),
        in_specs=in_specs,
        out_specs=out_specs,
        out_shape=out_shape,
        compiler_params=pltpu.CompilerParams(
            dimension_semantics=("parallel", "arbitrary"), vmem_limit_bytes=VMEM_LIMIT_BYTES),
        name="in_proj",
    )(x, w, *([w_gates] if with_gates else []))


def _mlstm_body(q_ref, k_ref, v_ref, og_ref, g_ref, bg_ref, conv_ref, norm_ref, y_ref,
                tail_ref, shift_ref, qk_ref, c_ref, n_ref, m_ref):
    ln = MLSTM_CHUNK
    dh = HEAD_DIM_M
    taps = CONV_W - 1

    @pl.when(pl.program_id(1) == 0)
    def _():
        tail_ref[0:SUBLANES, :] = jnp.zeros((SUBLANES, 2 * D_M), F32)
        c_ref[...] = jnp.zeros_like(c_ref)
        n_ref[...] = jnp.zeros_like(n_ref)
        m_ref[...] = jnp.full_like(m_ref, NEG_BIG)
        r = lax.broadcasted_iota(jnp.int32, (taps * ln, ln), 0)
        s = lax.broadcasted_iota(jnp.int32, (taps * ln, ln), 1)
        shift_ref[...] = jnp.where(s == (r & (ln - 1)) - (r // ln + 1), 1.0, 0.0).astype(BF16)

    for part, x_ref in enumerate((q_ref, k_ref)):
        for h in range(N_HEADS_M):
            cs = slice(part * D_M + h * dh, part * D_M + (h + 1) * dh)
            x = x_ref[:, h * dh:(h + 1) * dh]
            sh = _dot(shift_ref[...], x)
            acc = sh[(taps - 1) * ln:taps * ln] * conv_ref[0:1, cs]
            for j in range(1, taps):
                acc = acc + sh[(taps - 1 - j) * ln:(taps - j) * ln] * conv_ref[j:j + 1, cs]
            qk_ref[:, cs] = acc + x.astype(F32) * conv_ref[taps:taps + 1, cs]
    tail_ref[SUBLANES:2 * SUBLANES, 0:D_M] = q_ref[0:2 * SUBLANES, :].astype(F32)[0:SUBLANES]
    tail_ref[SUBLANES:2 * SUBLANES, D_M:2 * D_M] = k_ref[0:2 * SUBLANES, :].astype(F32)[0:SUBLANES]
    base = SUBLANES - taps
    first = tail_ref[base:base + SUBLANES, :] * conv_ref[0:1, :]
    for j in range(1, CONV_W):
        first = first + tail_ref[base + j:base + j + SUBLANES, :] * conv_ref[j:j + 1, :]
    qk_ref[0:SUBLANES, :] = first
    tail_ref[0:SUBLANES, 0:D_M] = q_ref[ln - 2 * SUBLANES:ln, :].astype(F32)[SUBLANES:2 * SUBLANES]
    tail_ref[0:SUBLANES, D_M:2 * D_M] = k_ref[ln - 2 * SUBLANES:ln, :].astype(F32)[SUBLANES:2 * SUBLANES]

    def conv_silu(col0):
        acc = qk_ref[:, col0:col0 + dh]
        return acc * _sigmoid(acc)

    gall = g_ref[...] + bg_ref[...]
    lane = lax.broadcasted_iota(jnp.int32, (ln, GATE_COLS), 1)
    is_f = (lane >= N_HEADS_M) & (lane < 2 * N_HEADS_M)
    lf = jnp.where(is_f, _log_sigmoid(gall), 0.0)
    row = lax.broadcasted_iota(jnp.int32, (ln, ln), 0)
    col = lax.broadcasted_iota(jnp.int32, (ln, ln), 1)
    causal = col <= row
    tri = jnp.where(causal, 1.0, 0.0).astype(BF16)
    lf_hi, lf_lo = _split_bf16(lf)
    b_all = _dot(tri, lf_hi) + _dot(tri, lf_lo)
    b_all_t = b_all.T
    gall_t = gall.T

    for h in range(N_HEADS_M):
        hs = slice(h * dh, (h + 1) * dh)
        qh = conv_silu(h * dh)
        kh = conv_silu(D_M + h * dh) * (HEAD_DIM_M ** -0.5)
        qb = qh.astype(BF16)
        kb = kh.astype(BF16)
        vb = v_ref[:, hs]

        b_col = b_all[:, N_HEADS_M + h:N_HEADS_M + h + 1]
        li_col = gall[:, h:h + 1]
        b_row = b_all_t[N_HEADS_M + h:N_HEADS_M + h + 1, :]
        li_row = gall_t[h:h + 1, :]
        m_prev = m_ref[h:h + 1, 0:1]

        d_log = jnp.where(causal, (b_col - b_row) + li_row, NEG_BIG)
        inter_log = b_col + m_prev
        m_t = jnp.maximum(jnp.max(d_log, axis=-1, keepdims=True), inter_log)
        scores = _dot_nt(qb, kb) * jnp.exp(d_log - m_t)
        inter_scale = jnp.exp(inter_log - m_t)
        c_prev = c_ref[h]
        n_prev = n_ref[h]
        num = _dot(scores.astype(BF16), vb) + inter_scale * _dot(qb, c_prev.astype(BF16))
        den = (jnp.sum(scores, axis=-1, keepdims=True)
               + inter_scale * jnp.sum(qh * n_prev, axis=-1, keepdims=True))
        hval = num / jnp.maximum(jnp.abs(den), jnp.exp(-m_t))

        g_last = b_col[ln - 1:ln, :]
        w_log = (g_last - b_col) + li_col
        m_new = jnp.maximum(g_last + m_prev, jnp.max(w_log, axis=0, keepdims=True))
        decay = jnp.exp((g_last + m_prev) - m_new)
        kw = kh * jnp.exp(w_log - m_new)
        c_ref[h] = decay * c_prev + _dot_tn(kw.astype(BF16), vb)
        n_ref[h] = decay * n_prev + jnp.sum(kw, axis=0, keepdims=True)
        m_ref[h:h + 1, :] = jnp.broadcast_to(m_new, (1, LANES))

        mu = jnp.mean(hval, axis=-1, keepdims=True)
        hc = hval - mu
        var = jnp.mean(hc * hc, axis=-1, keepdims=True)
        hn = hc * lax.rsqrt(var + LN_EPS) * norm_ref[:, hs]
        y_ref[:, hs] = (hn * _sigmoid(og_ref[:, hs].astype(F32))).astype(BF16)


def _mlstm(proj, gates, b_gates, conv_w, norm_g, bsz, seq):
    ln = MLSTM_CHUNK
    n_chunks = seq // ln
    col_blk = 0
    return pl.pallas_call(
        _mlstm_body,
        grid=(bsz, n_chunks),
        in_specs=[
            pl.BlockSpec((None, ln, D_M), lambda b, c: (b, c, col_blk)),
            pl.BlockSpec((None, ln, D_M), lambda b, c: (b, c, col_blk + 1)),
            pl.BlockSpec((None, ln, D_M), lambda b, c: (b, c, col_blk + 2)),
            pl.BlockSpec((None, ln, D_M), lambda b, c: (b, c, col_blk + 3)),
            pl.BlockSpec((None, ln, GATE_COLS), lambda b, c: (b, c, 0)),
            pl.BlockSpec((1, GATE_COLS), lambda b, c: (0, 0)),
            pl.BlockSpec((CONV_W, 2 * D_M), lambda b, c: (0, 0)),
            pl.BlockSpec((1, D_M), lambda b, c: (0, 0)),
        ],
        out_specs=pl.BlockSpec((None, ln, D_M), lambda b, c: (b, c, 0)),
        out_shape=jax.ShapeDtypeStruct((bsz, seq, D_M), BF16),
        scratch_shapes=[
            pltpu.VMEM((2 * SUBLANES, 2 * D_M), F32),
            pltpu.VMEM(((CONV_W - 1) * ln, ln), BF16),
            pltpu.VMEM((ln, 2 * D_M), F32),
            pltpu.VMEM((N_HEADS_M, HEAD_DIM_M, HEAD_DIM_M), F32),
            pltpu.VMEM((N_HEADS_M, 1, HEAD_DIM_M), F32),
            pltpu.VMEM((SUBLANES, LANES), F32),
        ],
        compiler_params=pltpu.CompilerParams(
            dimension_semantics=("parallel", "arbitrary"), vmem_limit_bytes=VMEM_LIMIT_BYTES),
        name="mlstm",
    )(proj, proj, proj, proj, gates, b_gates, conv_w, norm_g)


def _sb_body(q_ref, k_ref, v_ref, o_ref):
    tq = SB_TQ
    dh = HEAD_DIM_SB
    nh = SB_HEADS_PER_STEP
    qi = pl.program_id(2)
    z_scale = (HEAD_DIM_SB ** -0.5) * LOG2_E
    row = lax.broadcasted_iota(jnp.int32, (nh * tq, tq), 0) & (tq - 1)
    col = lax.broadcasted_iota(jnp.int32, (nh * tq, tq), 1)
    strict = col < row
    row2 = lax.broadcasted_iota(jnp.int32, (2 * tq, tq), 0)
    col2 = lax.broadcasted_iota(jnp.int32, (2 * tq, tq), 1)
    suffix2 = jnp.where((row2 & (tq - 1)) >= col2, 1.0, 0.0).astype(BF16)
    qs = [q_ref[:, hh * dh:(hh + 1) * dh] for hh in range(nh)]

    def block(kb, carry, accs, keep=None):
        off = pl.multiple_of(kb * tq, tq)
        z2 = jnp.concatenate(
            [_dot_nt(qs[hh], k_ref[pl.ds(off, tq), hh * dh:(hh + 1) * dh]) for hh in range(nh)], axis=0) * z_scale
        if keep is not None:
            z2 = jnp.where(keep, z2, NEG_BIG)
        neg_part = jnp.minimum(z2, 0.0)
        neg_relu = neg_part - z2
        lom2 = neg_relu - jnp.log2(1.0 + jnp.exp2(neg_part + neg_relu))
        hi, lo = _split_bf16(lom2)
        rem = _dot(jnp.concatenate([hi, lo], axis=1), suffix2)
        att = jnp.exp2(z2 + (rem + carry)).astype(BF16)
        accs = tuple(
            accs[hh] + _dot(att[hh * tq:(hh + 1) * tq], v_ref[pl.ds(off, tq), hh * dh:(hh + 1) * dh])
            for hh in range(nh))
        return carry + rem[:, 0:1], accs

    carry, accs = block(qi, jnp.zeros((nh * tq, 1), F32),
                        tuple(jnp.zeros((tq, dh), F32) for _ in range(nh)), strict)
    carry, accs = block(jnp.maximum(qi - 1, 0), jnp.where(qi > 0, carry, NEG_BIG), accs)

    def cond(loop):
        j, cmax = loop[0], loop[1]
        return jnp.logical_and(j < qi, cmax > SB_SKIP_LOG2)

    def step(loop):
        j = loop[0]
        new_carry, new_accs = block(qi - 1 - j, loop[2], loop[3:], None)
        return (j + 1, jnp.max(new_carry), new_carry) + new_accs

    final = lax.while_loop(cond, step, (jnp.int32(1), jnp.max(carry), carry) + accs)[3:]
    for hh in range(nh):
        o_ref[:, hh * dh:(hh + 1) * dh] = final[hh].astype(BF16)


def _stick_breaking(proj, bsz, seq):
    tq = SB_TQ
    width = SB_HEADS_PER_STEP * HEAD_DIM_SB
    q_blk = 0
    k_blk = q_blk + D_SB // width
    v_blk = k_blk + D_SB // width
    return pl.pallas_call(
        _sb_body,
        grid=(bsz, D_SB // width, seq // tq),
        in_specs=[
            pl.BlockSpec((None, tq, width), lambda b, h, i: (b, i, q_blk + h)),
            pl.BlockSpec((None, seq, width), lambda b, h, i: (b, 0, k_blk + h)),
            pl.BlockSpec((None, seq, width), lambda b, h, i: (b, 0, v_blk + h)),
        ],
        out_specs=pl.BlockSpec((None, tq, width), lambda b, h, i: (b, i, h)),
        out_shape=jax.ShapeDtypeStruct((bsz, seq, D_SB), BF16),
        compiler_params=pltpu.CompilerParams(
            dimension_semantics=("parallel", "parallel", "arbitrary"), vmem_limit_bytes=VMEM_LIMIT_BYTES),
        name="stickbreak",
    )(proj, proj, proj)


def _merge_ln_body(ym_ref, ysb_ref, ga_lo_ref, ga_hi_ref, gb_lo_ref, gb_hi_ref, x_ref, wum_ref, wus_ref, wo_ref,
                   g_ref, b_ref, o_ref, y_even_ref, y_odd_ref):
    i = pl.program_id(0)
    n_tiles = pl.num_programs(0) - 1

    def ln_of_previous_tile(y_prev_ref):
        o_ref[...] = _layer_norm(y_prev_ref[...], g_ref[...], b_ref[...])

    def tile_step(y_ref, y_prev_ref):
        half = D_MODEL // 2
        mix = None
        for c, (ga_ref, gb_ref) in enumerate(((ga_lo_ref, gb_lo_ref), (ga_hi_ref, gb_hi_ref))):
            cs = slice(c * half, (c + 1) * half)
            um = _dot(ym_ref[...], wum_ref[:, cs])
            us = _dot(ysb_ref[...], wus_ref[:, cs])
            merged = _sigmoid(ga_ref[...].astype(F32)) * um + _sigmoid(gb_ref[...].astype(F32)) * us
            part = _dot(merged.astype(BF16), wo_ref[cs, :])
            mix = part if mix is None else mix + part
        y_ref[...] = ALPHA * x_ref[...] + mix
        ln_of_previous_tile(y_prev_ref)

    @pl.when(i == 0)
    def _():
        y_odd_ref[...] = jnp.zeros_like(y_odd_ref)

    is_even = i % 2 == 0

    @pl.when(jnp.logical_and(i < n_tiles, is_even))
    def _():
        tile_step(y_even_ref, y_odd_ref)

    @pl.when(jnp.logical_and(i < n_tiles, jnp.logical_not(is_even)))
    def _():
        tile_step(y_odd_ref, y_even_ref)

    @pl.when(jnp.logical_and(i == n_tiles, is_even))
    def _():
        ln_of_previous_tile(y_odd_ref)

    @pl.when(jnp.logical_and(i == n_tiles, jnp.logical_not(is_even)))
    def _():
        ln_of_previous_tile(y_even_ref)


def _merge_ln(ym, ysb, proj, x, w_up_m, w_up_sb, w_out, g, b):
    t = x.shape[0]
    tm = MERGE_TM
    n_tiles = t // tm
    const = dict(pipeline_mode=pl.Buffered(1))
    cur = lambda i: jnp.minimum(i, n_tiles - 1)
    half = D_MODEL // 2
    gate_blk = PROJ_S_GATE_COL // half
    return pl.pallas_call(
        _merge_ln_body,
        grid=(n_tiles + 1,),
        in_specs=[
            pl.BlockSpec((tm, D_M), lambda i: (cur(i), 0)),
            pl.BlockSpec((tm, D_SB), lambda i: (cur(i), 0)),
            *[pl.BlockSpec((tm, half), lambda i, c=gate_blk + c: (cur(i), c)) for c in range(4)],
            pl.BlockSpec((tm, D_MODEL), lambda i: (cur(i), 0)),
            pl.BlockSpec((D_M, D_MODEL), lambda i: (0, 0), **const),
            pl.BlockSpec((D_SB, D_MODEL), lambda i: (0, 0), **const),
            pl.BlockSpec((D_MODEL, D_MODEL), lambda i: (0, 0), **const),
            pl.BlockSpec((1, D_MODEL), lambda i: (0, 0)),
            pl.BlockSpec((1, D_MODEL), lambda i: (0, 0)),
        ],
        out_specs=pl.BlockSpec((tm, D_MODEL), lambda i: (jnp.maximum(i - 1, 0), 0)),
        out_shape=jax.ShapeDtypeStruct((t, D_MODEL), F32),
        scratch_shapes=[pltpu.VMEM((tm, D_MODEL), F32), pltpu.VMEM((tm, D_MODEL), F32)],
        compiler_params=pltpu.CompilerParams(
            dimension_semantics=("arbitrary",), vmem_limit_bytes=VMEM_LIMIT_BYTES),
        name="merge_ln",
    )(ym, ysb, proj, proj, proj, proj, x, w_up_m, w_up_sb, w_out, g, b)


def _ple_body(x_ref, p_ref, wg_ref, wp_ref, o_ref):
    x = x_ref[...]
    gate = _sigmoid(_dot(x.astype(BF16), wg_ref[...]))
    o_ref[...] = x + gate * _dot(p_ref[...].astype(BF16), wp_ref[...])


def _ple(x, p, w_gate, w_proj):
    t = x.shape[0]
    tm = PLE_TM
    const = dict(pipeline_mode=pl.Buffered(1))
    return pl.pallas_call(
        _ple_body,
        grid=(t // tm,),
        in_specs=[
            pl.BlockSpec((tm, D_MODEL), lambda i: (i, 0)),
            pl.BlockSpec((tm, D_PLE), lambda i: (i, 0)),
            pl.BlockSpec((D_MODEL, D_MODEL), lambda i: (0, 0), **const),
            pl.BlockSpec((D_PLE, D_MODEL), lambda i: (0, 0), **const),
        ],
        out_specs=pl.BlockSpec((tm, D_MODEL), lambda i: (i, 0)),
        out_shape=jax.ShapeDtypeStruct((t, D_MODEL), F32),
        compiler_params=pltpu.CompilerParams(
            dimension_semantics=("parallel",), vmem_limit_bytes=VMEM_LIMIT_BYTES),
        name="ple",
    )(x, p, w_gate, w_proj)


def kernel(x, p, ffn1_w1, ffn1_w3, ffn1_w2, ln1_g, ln1_b, w_in, b_gates_m, conv_m, norm_m, w_up_m, w_up_sb, w_out, ln2_g, ln2_b, ffn2_w1, ffn2_w3, ffn2_w2, ln3_g, ln3_b, w_ple_gate, w_ple_proj):
    bsz, seq, _ = x.shape
    t = bsz * seq
    xf = x.reshape(t, D_MODEL)
    row = lambda v: v.reshape(1, -1)
    for i in range(DEPTH):
        xf, xb = _ffn_ln(xf, ffn1_w1[i].astype(BF16), ffn1_w3[i].astype(BF16), ffn1_w2[i].astype(BF16),
                         row(ln1_g[i]), row(ln1_b[i]), also_bf16=True)

        g0, g1 = PROJ_M_COLS, PROJ_M_COLS + 2 * N_HEADS_M
        gate_pad = jnp.zeros((D_MODEL, GATE_COLS - 2 * N_HEADS_M), BF16)
        w_gates = jnp.concatenate([w_in[i][:, g0:g1].astype(BF16), gate_pad], axis=1)
        b_gates = jnp.concatenate([b_gates_m[i], jnp.zeros((GATE_COLS - 2 * N_HEADS_M,), F32)]).reshape(1, GATE_COLS)
        proj_m, gates = _in_proj(xb, w_in[i][:, :g0].astype(BF16), w_gates)
        (proj_s,) = _in_proj(xb, w_in[i][:, g1:].astype(BF16))

        ym = _mlstm(proj_m.reshape(bsz, seq, PROJ_M_COLS), gates.reshape(bsz, seq, GATE_COLS), b_gates,
                    conv_m[i], row(norm_m[i]), bsz, seq)
        ysb = _stick_breaking(proj_s.reshape(bsz, seq, PROJ_S_COLS), bsz, seq)

        xf = _merge_ln(ym.reshape(t, D_M), ysb.reshape(t, D_SB), proj_s, xf,
                       w_up_m[i].astype(BF16), w_up_sb[i].astype(BF16), w_out[i].astype(BF16),
                       row(ln2_g[i]), row(ln2_b[i]))
        (xf,) = _ffn_ln(xf, ffn2_w1[i].astype(BF16), ffn2_w3[i].astype(BF16), ffn2_w2[i].astype(BF16),
                        row(ln3_g[i]), row(ln3_b[i]), also_bf16=False)
        xf = _ple(xf, p[i].reshape(t, D_PLE), w_ple_gate[i].astype(BF16), w_ple_proj[i].astype(BF16))
    return xf.reshape(bsz, seq, D_MODEL)
```
